```python
import math
import jax, jax.numpy as jnp
from jax import lax
import numpy as np

D_MODEL = 1024
BATCH = 32
SEQ = 2048
DEPTH = 2
DEC_BATCH = 32
DEC_SEQ = 32
PAST_LEN = 4096

CHUNK = 64
Q_BLOCK = 128
EPS = 1e-6
SB_HEADS = 8
SB_DIM = 64
SB_W = SB_HEADS * SB_DIM
MLA_HEADS = 8
MLA_NOPE = 64
MLA_ROPE = 32
MLA_V = 64
Q_RANK = 384
KV_RANK = 256
ROPE_THETA = 10000.0
MLA_W = MLA_HEADS * MLA_V
MIX_W = SB_W + MLA_W
SPLITS = (SB_W, 2 * SB_W, 3 * SB_W, 3 * SB_W + Q_RANK, 3 * SB_W + Q_RANK + KV_RANK)
IN_COLS = 3 * SB_W + Q_RANK + KV_RANK + MLA_ROPE
N_GROUPS = 4
EXP_PER_GROUP = 8
N_EXPERTS = N_GROUPS * EXP_PER_GROUP
TOP_K = 2
D_EXPERT = 256

kernel_name = 'streaming_hybrid_sb_mla_hmoe'


def rmsnorm(x, g):
    xf = x.astype(jnp.float32)
    y = xf * lax.rsqrt(jnp.mean(xf * xf, axis=-1, keepdims=True) + EPS)
    return (y * g.astype(jnp.float32)).astype(x.dtype)


def modulate(h, shift, scale):
    return h * (1 + scale[:, None, :]) + shift[:, None, :]


def rope_tables(pos):
    inv = ROPE_THETA ** (-jnp.arange(0, MLA_ROPE, 2, dtype=jnp.float32) / MLA_ROPE)
    ang = pos.astype(jnp.float32)[:, None] * inv[None, :]
    return jnp.cos(ang), jnp.sin(ang)


def rope(x, cos, sin):
    x1, x2 = jnp.split(x, 2, axis=-1)
    cos = cos.astype(x.dtype)
    sin = sin.astype(x.dtype)
    return jnp.concatenate([x1 * cos - x2 * sin, x1 * sin + x2 * cos], axis=-1)


def sb_attend(q, k, v, q_pos, k_pos):
    z = jnp.einsum('bqhd,bkhd->bhqk', q, k, preferred_element_type=jnp.float32) / math.sqrt(SB_DIM)
    allowed = k_pos[None, :] < q_pos[:, None]
    log_beta = jax.nn.log_sigmoid(z)
    log_1m = jnp.where(allowed, jax.nn.log_sigmoid(-z), 0.0)
    suffix = lax.cumsum(log_1m, axis=z.ndim - 1, reverse=True) - log_1m
    w = jnp.where(allowed, jnp.exp(log_beta + suffix), 0.0)
    return jnp.einsum('bhqk,bkhd->bqhd', w.astype(v.dtype), v)


def mla_attend(q, k, v, q_pos, k_pos):
    z = jnp.einsum('bqhd,bkhd->bhqk', q, k, preferred_element_type=jnp.float32) / math.sqrt(MLA_NOPE + MLA_ROPE)
    allowed = (k_pos // CHUNK)[None, :] <= (q_pos // CHUNK)[:, None]
    p = jax.nn.softmax(jnp.where(allowed, z, -1e30), axis=-1)
    return jnp.einsum('bhqk,bkhd->bqhd', p.astype(v.dtype), v)


def query_blocks(fn, q_sb, q_mla, pos):
    B, S = q_sb.shape[:2]
    if S <= Q_BLOCK:
        return fn(q_sb, q_mla, pos)
    nb = S // Q_BLOCK
    def split(a):
        return a.reshape(B, nb, Q_BLOCK, *a.shape[2:]).swapaxes(0, 1)
    out = lax.map(lambda args: fn(*args), (split(q_sb), split(q_mla), pos.reshape(nb, Q_BLOCK)))
    return out.swapaxes(0, 1).reshape(B, S, out.shape[-1])


def hier_moe(h, w_rg, b_rg, w_re, b_re, w_gate, w_up, w_down):
    def per_seq(hs):
        g_logits = jnp.dot(hs, w_rg, preferred_element_type=jnp.float32) + b_rg.astype(jnp.float32)
        g_prob = jax.nn.softmax(g_logits, axis=-1)
        g_sel = jnp.argmax(g_logits, axis=-1)
        p_g = jnp.take_along_axis(g_prob, g_sel[:, None], axis=-1)
        e_logits = jnp.dot(hs, w_re, preferred_element_type=jnp.float32) + b_re.astype(jnp.float32)
        e_logits = e_logits.reshape(hs.shape[0], N_GROUPS, EXP_PER_GROUP)
        e_in = jnp.take_along_axis(e_logits, g_sel[:, None, None], axis=1)[:, 0]
        top_v, top_i = lax.top_k(e_in, TOP_K)
        w2 = jax.nn.softmax(top_v, axis=-1) * p_g
        idx = g_sel[:, None] * EXP_PER_GROUP + top_i
        gates = jnp.einsum('ske,sk->se', jax.nn.one_hot(idx, N_EXPERTS, dtype=jnp.float32), w2)
        a = jnp.einsum('sd,edf->sef', hs, w_gate)
        b = jnp.einsum('sd,edf->sef', hs, w_up)
        act = jax.nn.silu(a) * b * gates[:, :, None].astype(a.dtype)
        return jnp.einsum('sef,efd->sd', act, w_down)
    return lax.map(per_seq, h)


def trunk_layer(x, c, pos, past, w):
    (w_ada, b_ada, g_mix, w_in, g_sb_q, g_sb_k, g_qa, w_qb, g_kva, w_kvb,
     g_mq_nope, g_mk_nope, g_mq_pe, g_mk_pe, w_out, g_ffn,
     w_rg, b_rg, w_re, b_re, w_gate, w_up, w_down) = w
    B, S, _ = x.shape
    mod = jnp.dot(jax.nn.silu(c), w_ada) + b_ada
    sh_a, sc_a, gt_a, sh_m, sc_m, gt_m = jnp.split(mod, 6, axis=-1)

    h = modulate(rmsnorm(x, g_mix), sh_a, sc_a)
    u = jnp.dot(h, w_in)
    q_sb, k_sb, v_sb, q_lat, kv_lat, kpe_raw = jnp.split(u, SPLITS, axis=-1)
    cos, sin = rope_tables(pos)
    q_sb = rmsnorm(q_sb.reshape(B, S, SB_HEADS, SB_DIM), g_sb_q)
    k_sb_new = rmsnorm(k_sb.reshape(B, S, SB_HEADS, SB_DIM), g_sb_k)
    v_sb_new = v_sb.reshape(B, S, SB_HEADS, SB_DIM)
    qm = jnp.dot(rmsnorm(q_lat, g_qa), w_qb).reshape(B, S, MLA_HEADS, MLA_NOPE + MLA_ROPE)
    q_nope = rmsnorm(qm[..., :MLA_NOPE], g_mq_nope)
    q_pe = rope(rmsnorm(qm[..., MLA_NOPE:], g_mq_pe), cos[:, None, :], sin[:, None, :])
    q_mla = jnp.concatenate([q_nope, q_pe], axis=-1)
    ckv_new = rmsnorm(kv_lat, g_kva)
    kpe_new = rope(rmsnorm(kpe_raw, g_mk_pe), cos, sin)

    if past is None:
        k_sb_all, v_sb_all, ckv_all, kpe_all = k_sb_new, v_sb_new, ckv_new, kpe_new
        k_pos = pos
    else:
        pk, pv, pc, pp = past
        k_sb_all = jnp.concatenate([pk, k_sb_new], axis=1)
        v_sb_all = jnp.concatenate([pv, v_sb_new], axis=1)
        ckv_all = jnp.concatenate([pc, ckv_new], axis=1)
        kpe_all = jnp.concatenate([pp, kpe_new], axis=1)
        k_pos = jnp.arange(pk.shape[1] + S, dtype=jnp.int32)
    T = ckv_all.shape[1]
    kv = jnp.dot(ckv_all, w_kvb).reshape(B, T, MLA_HEADS, MLA_NOPE + MLA_V)
    k_nope = rmsnorm(kv[..., :MLA_NOPE], g_mk_nope)
    v_mla = kv[..., MLA_NOPE:]
    k_mla = jnp.concatenate([k_nope, jnp.broadcast_to(kpe_all[:, :, None, :], (B, T, MLA_HEADS, MLA_ROPE))], axis=-1)

    def attend(qs_sb, qs_mla, qp):
        o_sb = sb_attend(qs_sb, k_sb_all, v_sb_all, qp, k_pos)
        o_mla = mla_attend(qs_mla, k_mla, v_mla, qp, k_pos)
        tq = qs_sb.shape[1]
        return jnp.concatenate([o_sb.reshape(B, tq, SB_W), o_mla.reshape(B, tq, MLA_W)], axis=-1)

    o = query_blocks(attend, q_sb, q_mla, pos)
    x = x + gt_a[:, None, :] * jnp.dot(o, w_out)

    h2 = modulate(rmsnorm(x, g_ffn), sh_m, sc_m)
    x = x + gt_m[:, None, :] * hier_moe(h2, w_rg, b_rg, w_re, b_re, w_gate, w_up, w_down)
    return x, (k_sb_new, v_sb_new, ckv_new, kpe_new)


def setup_inputs(seed: int = 0) -> dict:
    key = jax.random.key(seed)
    ks = iter(jax.random.split(key, 48))
    f32 = jnp.float32

    def nrm(shape, scale=1.0):
        return jax.random.normal(next(ks), shape, f32) * scale

    def gain(n):
        return 1.0 + nrm((DEPTH, n), 0.05)

    D = D_MODEL
    return {
        'x_prompt': nrm((BATCH, SEQ, D)),
        'x_sample': nrm((DEC_BATCH, DEC_SEQ, D)),
        'cache_sb_k': nrm((DEPTH, DEC_BATCH, PAST_LEN, SB_HEADS, SB_DIM)),
        'cache_sb_v': nrm((DEPTH, DEC_BATCH, PAST_LEN, SB_HEADS, SB_DIM)),
        'cache_mla_ckv': nrm((DEPTH, DEC_BATCH, PAST_LEN, KV_RANK)),
        'cache_mla_kpe': nrm((DEPTH, DEC_BATCH, PAST_LEN, MLA_ROPE)),
        'c_prompt': nrm((BATCH, D)),
        'c_sample': nrm((DEC_BATCH, D)),
        'w_ada': nrm((DEPTH, D, 6 * D), 0.5 * D ** -0.5),
        'b_ada': nrm((DEPTH, 6 * D), 0.02),
        'g_mix': gain(D),
        'w_in': nrm((DEPTH, D, IN_COLS), D ** -0.5),
        'g_sb_q': gain(SB_DIM),
        'g_sb_k': gain(SB_DIM),
        'g_qa': gain(Q_RANK),
        'w_qb': nrm((DEPTH, Q_RANK, MLA_HEADS * (MLA_NOPE + MLA_ROPE)), Q_RANK ** -0.5),
        'g_kva': gain(KV_RANK),
        'w_kvb': nrm((DEPTH, KV_RANK, MLA_HEADS * (MLA_NOPE + MLA_V)), KV_RANK ** -0.5),
        'g_mq_nope': gain(MLA_NOPE),
        'g_mk_nope': gain(MLA_NOPE),
        'g_mq_pe': gain(MLA_ROPE),
        'g_mk_pe': gain(MLA_ROPE),
        'w_out': nrm((DEPTH, MIX_W, D), MIX_W ** -0.5),
        'g_ffn': gain(D),
        'w_rg': nrm((DEPTH, D, N_GROUPS), D ** -0.5),
        'b_rg': nrm((DEPTH, N_GROUPS), 0.01),
        'w_re': nrm((DEPTH, D, N_EXPERTS), D ** -0.5),
        'b_re': nrm((DEPTH, N_EXPERTS), 0.01),
        'w_gate': nrm((DEPTH, N_EXPERTS, D, D_EXPERT), D ** -0.5),
        'w_up': nrm((DEPTH, N_EXPERTS, D, D_EXPERT), D ** -0.5),
        'w_down': nrm((DEPTH, N_EXPERTS, D_EXPERT, D), D_EXPERT ** -0.5),
    }


def reference(x_prompt, x_sample, cache_sb_k, cache_sb_v, cache_mla_ckv, cache_mla_kpe,
              c_prompt, c_sample, w_ada, b_ada, g_mix, w_in, g_sb_q, g_sb_k, g_qa, w_qb,
              g_kva, w_kvb, g_mq_nope, g_mk_nope, g_mq_pe, g_mk_pe, w_out, g_ffn,
              w_rg, b_rg, w_re, b_re, w_gate, w_up, w_down):
    weights = (w_ada, b_ada, g_mix, w_in, g_sb_q, g_sb_k, g_qa, w_qb, g_kva, w_kvb,
               g_mq_nope, g_mk_nope, g_mq_pe, g_mk_pe, w_out, g_ffn,
               w_rg, b_rg, w_re, b_re, w_gate, w_up, w_down)
    pos_p = jnp.arange(x_prompt.shape[1], dtype=jnp.int32)
    pos_s = cache_sb_k.shape[2] + jnp.arange(x_sample.shape[1], dtype=jnp.int32)
    yp, ys = x_prompt, x_sample
    st_p, st_s = [], []
    for l in range(DEPTH):
        wl = tuple(wt[l] for wt in weights)
        yp, sp = trunk_layer(yp, c_prompt, pos_p, None, wl)
        ys, ss = trunk_layer(ys, c_sample, pos_s,
                             (cache_sb_k[l], cache_sb_v[l], cache_mla_ckv[l], cache_mla_kpe[l]), wl)
        st_p.append(sp)
        st_s.append(ss)
    sb_k_prompt = jnp.stack([s[0] for s in st_p])
    sb_v_prompt = jnp.stack([s[1] for s in st_p])
    ckv_prompt = jnp.stack([s[2] for s in st_p])
    kpe_prompt = jnp.stack([s[3] for s in st_p])
    sb_k_sample = jnp.stack([s[0] for s in st_s])
    sb_v_sample = jnp.stack([s[1] for s in st_s])
    ckv_sample = jnp.stack([s[2] for s in st_s])
    kpe_sample = jnp.stack([s[3] for s in st_s])
    return (yp, ys, sb_k_prompt, sb_v_prompt, ckv_prompt, kpe_prompt,
            sb_k_sample, sb_v_sample, ckv_sample, kpe_sample)
```

```python
import functools
import math

import jax
import jax.numpy as jnp
from jax import lax
from jax.experimental import pallas as pl
from jax.experimental.pallas import tpu as pltpu

F32 = jnp.float32
BF16 = jnp.bfloat16

EPS = 1e-6
CHUNK = 64
SB_HEADS = 8
SB_DIM = 64
SB_W = SB_HEADS * SB_DIM
MLA_HEADS = 8
MLA_NOPE = 64
MLA_ROPE = 32
MLA_V = 64
Q_RANK = 384
KV_RANK = 256
ROPE_THETA = 10000.0
MLA_W = MLA_HEADS * MLA_V
N_GROUPS = 4
EXP_PER_GROUP = 8
N_EXPERTS = N_GROUPS * EXP_PER_GROUP
D_EXPERT = 256

LANES = 128
PAIR_W = 2 * SB_DIM
N_PAIRS = SB_HEADS // 2
KPE_TILE = LANES // MLA_ROPE
PE_W = MLA_HEADS * MLA_ROPE
C_QSB, C_KSB, C_VSB = 0, SB_W, 2 * SB_W
C_QLAT = 3 * SB_W
C_KVLAT = C_QLAT + Q_RANK
C_KPE = C_KVLAT + KV_RANK
IN_COLS_PAD = C_KPE + LANES
MLA_SCALE = 1.0 / math.sqrt(MLA_NOPE + MLA_ROPE)
NEG_BIG = -1e30
VMEM_LIMIT = 56 * 1024 * 1024


def _nt_dot(a, b):
    return lax.dot_general(a, b, (((1,), (1,)), ((), ())), preferred_element_type=F32)


def _dot(a, b):
    return jnp.dot(a, b, preferred_element_type=F32)


def _rms(x, g):
    ms = jnp.mean(x * x, axis=-1, keepdims=True)
    return x * lax.rsqrt(ms + EPS) * g


def _seg_rms(x, seg_mean, g):
    ms = _dot((x * x).astype(BF16), seg_mean)
    return x * lax.rsqrt(ms + EPS) * g


def _rot_half(x, lane_in_group_lt_half):
    w = x.shape[-1]
    fwd = pltpu.roll(x, w - MLA_ROPE // 2, axis=1)
    bwd = pltpu.roll(x, MLA_ROPE // 2, axis=1)
    return jnp.where(lane_in_group_lt_half, fwd, bwd)


def _ada_kernel(c_ref, w_ref, b_ref, o_ref):
    c = c_ref[...]
    s = c / (1.0 + jnp.exp(-c))
    o_ref[0] = _dot(s.astype(BF16), w_ref[0].astype(BF16)) + b_ref[0]


def _ada(c_all, w_ada, b_ada):
    depth, d, n = w_ada.shape
    nb = c_all.shape[0]
    tn = d
    return pl.pallas_call(
        _ada_kernel,
        out_shape=jax.ShapeDtypeStruct((depth, nb, n), F32),
        grid=(depth, n // tn),
        in_specs=[
            pl.BlockSpec((nb, d), lambda l, j: (0, 0)),
            pl.BlockSpec((1, d, tn), lambda l, j: (l, 0, j)),
            pl.BlockSpec((1, 1, tn), lambda l, j: (l, 0, j)),
        ],
        out_specs=pl.BlockSpec((1, nb, tn), lambda l, j: (l, 0, j)),
        name="ada",
        compiler_params=pltpu.CompilerParams(dimension_semantics=("arbitrary", "arbitrary")),
    )(c_all, w_ada, b_ada.reshape(depth, 1, n))


def _pre_kernel(x_ref, mod_ref, cos_ref, sin_ref, gmix_ref, win_ref, gsq_ref, gsk_ref, gqa_ref,
                wqn_ref, wqp_ref, gkva_ref, wkk_ref, wkv_ref, gqn_ref, gkn_ref, gqpe_ref, gkpe_ref,
                m64_ref, m32_ref,
                ksb_o, vsb_o, ckv_o, kpe_o,
                qsb_b, ksb_b, vsb_b, qn_b, qpe_b, kn_b, vm_b, kpe_b):
    nb, ts, d = x_ref.shape
    r = nb * ts
    x = x_ref[...]
    sh = mod_ref[:, 0:1, :]
    sc = mod_ref[:, 1:2, :]
    h = _rms(x, gmix_ref[...]) * (1.0 + sc) + sh
    u = _dot(h.reshape(r, d).astype(BF16), win_ref[...])

    m64 = m64_ref[...]
    q_sb = _seg_rms(u[:, C_QSB:C_QSB + SB_W], m64, gsq_ref[...])
    k_sb = _seg_rms(u[:, C_KSB:C_KSB + SB_W], m64, gsk_ref[...])
    v_sb = u[:, C_VSB:C_VSB + SB_W]
    ksb_o[...] = k_sb.reshape(nb, ts, SB_W)
    vsb_o[...] = v_sb.reshape(nb, ts, SB_W)
    qsb_b[...] = (q_sb * (1.0 / math.sqrt(SB_DIM))).astype(BF16).reshape(nb, ts, SB_W)
    ksb_b[...] = k_sb.astype(BF16).reshape(nb, ts, SB_W)
    vsb_b[...] = v_sb.astype(BF16).reshape(nb, ts, SB_W)

    cos = cos_ref[...]
    sin = sin_ref[...]
    lane = lax.broadcasted_iota(jnp.int32, (1, PE_W), 1)
    first_half = (lane & (MLA_ROPE - 1)) < (MLA_ROPE // 2)

    q_lat = _rms(u[:, C_QLAT:C_QLAT + Q_RANK], gqa_ref[...]).astype(BF16)
    qn = _seg_rms(_dot(q_lat, wqn_ref[...]), m64, gqn_ref[...])
    qn_b[...] = qn.astype(BF16).reshape(nb, ts, MLA_W)
    qp = _seg_rms(_dot(q_lat, wqp_ref[...]), m32_ref[...], gqpe_ref[...])
    qp_rot = _rot_half(qp, first_half)
    qp = qp.reshape(nb, ts, PE_W) * cos + qp_rot.reshape(nb, ts, PE_W) * sin
    qpe_b[...] = qp.astype(BF16)

    ckv = _rms(u[:, C_KVLAT:C_KVLAT + KV_RANK], gkva_ref[...])
    ckv_o[...] = ckv.reshape(nb, ts, KV_RANK)
    ckv16 = ckv.astype(BF16)
    kn = _seg_rms(_dot(ckv16, wkk_ref[...]), m64, gkn_ref[...])
    kn_b[...] = kn.astype(BF16).reshape(nb, ts, MLA_W)
    vm_b[...] = _dot(ckv16, wkv_ref[...]).astype(BF16).reshape(nb, ts, MLA_W)

    kp = _rms(u[:, C_KPE:C_KPE + LANES], gkpe_ref[...])
    kp_rot = _rot_half(kp, first_half[:, :LANES])
    kp = kp.reshape(nb, ts, LANES) * cos[:, :, :LANES] + kp_rot.reshape(nb, ts, LANES) * sin[:, :, :LANES]
    kpe_o[...] = kp[:, :, :MLA_ROPE]
    kpe_b[...] = kp.astype(BF16)


def _pre(x, mod, cos, sin, w, nb, ts):
    b, s, d = x.shape
    grid = (b // nb, s // ts)
    tok = lambda width: pl.BlockSpec((nb, ts, width), lambda i, j: (i, j, 0))
    const = lambda a: pl.BlockSpec(a.shape, lambda i, j: (0,) * a.ndim)
    weights = [w["g_mix"], w["w_in"], w["g_sb_q"], w["g_sb_k"], w["g_qa"], w["w_qn"], w["w_qp"], w["g_kva"],
               w["w_kk"], w["w_kv"], w["g_mq_nope"], w["g_mk_nope"], w["g_mq_pe"], w["g_mk_pe"],
               w["m64"], w["m32"]]
    f32o = lambda width: jax.ShapeDtypeStruct((b, s, width), F32)
    b16o = lambda width: jax.ShapeDtypeStruct((b, s, width), BF16)
    out_shape = (f32o(SB_W), f32o(SB_W), f32o(KV_RANK), f32o(MLA_ROPE),
                 b16o(SB_W), b16o(SB_W), b16o(SB_W), b16o(MLA_W), b16o(PE_W), b16o(MLA_W), b16o(MLA_W),
                 b16o(LANES))
    return pl.pallas_call(
        _pre_kernel,
        out_shape=out_shape,
        grid=grid,
        in_specs=[tok(d),
                  pl.BlockSpec((nb, 6, d), lambda i, j: (i, 0, 0)),
                  pl.BlockSpec((1, ts, PE_W), lambda i, j: (0, j, 0)),
                  pl.BlockSpec((1, ts, PE_W), lambda i, j: (0, j, 0))] + [const(a) for a in weights],
        out_specs=tuple(tok(o.shape[-1]) for o in out_shape),
        name="pre",
        compiler_params=pltpu.CompilerParams(dimension_semantics=("arbitrary", "arbitrary"),
                                             vmem_limit_bytes=VMEM_LIMIT),
    )(x, mod, cos, sin, *weights)


def _expand_kernel(ckv_ref, kpe_ref, wkk_ref, wkv_ref, gkn_ref, m64_ref, rep_ref, kn_b, vm_b, kpe_b):
    ckv16 = ckv_ref[0].astype(BF16)
    kn = _seg_rms(_dot(ckv16, wkk_ref[...]), m64_ref[...], gkn_ref[...])
    kn_b[0] = kn.astype(BF16)
    vm_b[0] = _dot(ckv16, wkv_ref[...]).astype(BF16)
    kpe_b[0] = _dot(kpe_ref[0].astype(BF16), rep_ref[...]).astype(BF16)


def _expand(ckv, kpe, w, tp):
    b, p, _ = ckv.shape
    const = lambda a: pl.BlockSpec(a.shape, lambda i, j: (0,) * a.ndim)
    weights = [w["w_kk"], w["w_kv"], w["g_mk_nope"], w["m64"], w["rep"]]
    return pl.pallas_call(
        _expand_kernel,
        out_shape=(jax.ShapeDtypeStruct((b, p, MLA_W), BF16), jax.ShapeDtypeStruct((b, p, MLA_W), BF16),
                   jax.ShapeDtypeStruct((b, p, LANES), BF16)),
        grid=(b, p // tp),
        in_specs=[pl.BlockSpec((1, tp, KV_RANK), lambda i, j: (i, j, 0)),
                  pl.BlockSpec((1, tp, MLA_ROPE), lambda i, j: (i, j, 0))] + [const(a) for a in weights],
        out_specs=(pl.BlockSpec((1, tp, MLA_W), lambda i, j: (i, j, 0)),
                   pl.BlockSpec((1, tp, MLA_W), lambda i, j: (i, j, 0)),
                   pl.BlockSpec((1, tp, LANES), lambda i, j: (i, j, 0))),
        name="expand",
        compiler_params=pltpu.CompilerParams(dimension_semantics=("arbitrary", "arbitrary"),
                                             vmem_limit_bytes=VMEM_LIMIT),
    )(ckv, kpe, *weights)


def _sb_block(qh, k, v, tri, carry, mask):
    z = _nt_dot(qh, k)
    lp = jnp.log1p(jnp.exp(-jnp.abs(z)))
    log_beta = jnp.minimum(z, 0.0) - lp
    log_1m = log_beta - z
    if mask is not None:
        log_1m = jnp.where(mask, log_1m, 0.0)
    hi = log_1m.astype(BF16)
    lo = (log_1m - hi.astype(F32)).astype(BF16)
    suffix = _dot(hi, tri) + _dot(lo, tri)
    wgt = jnp.exp(log_beta + suffix + carry)
    if mask is not None:
        wgt = jnp.where(mask, wgt, 0.0)
    pv = _dot(wgt.astype(BF16), v)
    return pv, carry + jnp.sum(log_1m, axis=-1, keepdims=True)


def _mla_scores(qcat, kcat, mask):
    z = _nt_dot(qcat, kcat) * MLA_SCALE
    if mask is not None:
        z = jnp.where(mask, z, NEG_BIG)
    return z


def _head_operands(qsb_ref, qn_ref, qpe_ref, h):
    p, e = divmod(h, 2)
    lane = lax.broadcasted_iota(jnp.int32, (1, LANES), 1)
    half = (lane >= SB_DIM) if e == 1 else (lane < SB_DIM)
    sl = slice(PAIR_W * p, PAIR_W * (p + 1))
    q_sb = jnp.where(half, qsb_ref[:, sl], jnp.zeros((), BF16))
    q_n = jnp.where(half, qn_ref[:, sl], jnp.zeros((), BF16))
    slab, pos = divmod(h, KPE_TILE)
    pe_lanes = (lane >= MLA_ROPE * pos) & (lane < MLA_ROPE * (pos + 1))
    q_pe = jnp.where(pe_lanes, qpe_ref[:, LANES * slab:LANES * (slab + 1)], jnp.zeros((), BF16))
    return q_sb, jnp.concatenate([q_n, q_pe], axis=-1)


def _attend_block(q_ops, k_sb, v_sb, k_n, k_pe, v_m, tri, sb_mask, mla_mask, first,
                  acc_sb, acc_m, c_s, m_s, l_s):
    for p in range(N_PAIRS):
        sl = slice(PAIR_W * p, PAIR_W * (p + 1))
        ks, vs, vm = k_sb[:, sl], v_sb[:, sl], v_m[:, sl]
        kcat = jnp.concatenate([k_n[:, sl], k_pe], axis=-1)
        for e in range(2):
            h = 2 * p + e
            q_sb, q_cat = q_ops[h]
            carry = jnp.zeros((q_sb.shape[0], 1), F32) if first else c_s[h]
            pv, carry = _sb_block(q_sb, ks, vs, tri, carry, sb_mask)
            c_s[h] = carry
            z = _mla_scores(q_cat, kcat, mla_mask)
            zmax = jnp.max(z, axis=-1, keepdims=True)
            if first:
                acc_sb[h] = pv
                prob = jnp.exp(z - zmax)
                m_s[h] = zmax
                l_s[h] = jnp.sum(prob, axis=-1, keepdims=True)
                acc_m[h] = _dot(prob.astype(BF16), vm)
            else:
                acc_sb[h] += pv
                m_old = m_s[h]
                m_new = jnp.maximum(m_old, zmax)
                prob = jnp.exp(z - m_new)
                alpha = jnp.exp(m_old - m_new)
                m_s[h] = m_new
                l_s[h] = alpha * l_s[h] + jnp.sum(prob, axis=-1, keepdims=True)
                acc_m[h] = alpha * acc_m[h] + _dot(prob.astype(BF16), vm)


def _write_out(o_ref, acc_sb, acc_m, l_s):
    lane = lax.broadcasted_iota(jnp.int32, (1, LANES), 1)
    lo_half = lane < SB_DIM
    for p in range(N_PAIRS):
        h0, h1 = 2 * p, 2 * p + 1
        o_ref[:, PAIR_W * p:PAIR_W * (p + 1)] = jnp.where(lo_half, acc_sb[h0], acc_sb[h1]).astype(o_ref.dtype)
        mla = jnp.where(lo_half, acc_m[h0] / l_s[h0], acc_m[h1] / l_s[h1])
        o_ref[:, SB_W + PAIR_W * p:SB_W + PAIR_W * (p + 1)] = mla.astype(o_ref.dtype)


def _diag_masks(tq, tk, q_pos0, k_pos0):
    qpos = q_pos0 + lax.broadcasted_iota(jnp.int32, (tq, tk), 0)
    kpos = k_pos0 + lax.broadcasted_iota(jnp.int32, (tq, tk), 1)
    shift = CHUNK.bit_length() - 1
    return kpos < qpos, (kpos >> shift) <= (qpos >> shift)


def _attn_prompt_kernel(qsb_ref, qn_ref, qpe_ref, ksb_ref, vsb_ref, kn_ref, kpe_ref, vm_ref, tri_ref,
                        o_ref, acc_sb, acc_m, c_s, m_s, l_s):
    t = qsb_ref.shape[1]
    qi = pl.program_id(1)
    q_ops = [_head_operands(qsb_ref.at[0], qn_ref.at[0], qpe_ref.at[0], h) for h in range(SB_HEADS)]
    tri = tri_ref[...]

    def key_block(start):
        rows = pl.ds(start, t)
        return (ksb_ref[0, rows, :], vsb_ref[0, rows, :], kn_ref[0, rows, :], kpe_ref[0, rows, :],
                vm_ref[0, rows, :])

    sb_mask, mla_mask = _diag_masks(t, t, 0, 0)
    k_sb, v_sb, k_n, k_pe, v_m = key_block(pl.multiple_of(qi * t, t))
    _attend_block(q_ops, k_sb, v_sb, k_n, k_pe, v_m, tri, sb_mask, mla_mask, True,
                  acc_sb, acc_m, c_s, m_s, l_s)

    def body(j, _):
        kb = qi - 1 - j
        k_sb, v_sb, k_n, k_pe, v_m = key_block(pl.multiple_of(kb * t, t))
        _attend_block(q_ops, k_sb, v_sb, k_n, k_pe, v_m, tri, None, None, False,
                      acc_sb, acc_m, c_s, m_s, l_s)
        return 0

    lax.fori_loop(0, qi, body, 0)
    _write_out(o_ref.at[0], acc_sb, acc_m, l_s)


def _attn_prompt(qsb, qn, qpe, ksb, vsb, kn, kpe, vm, tri):
    b, s, _ = qsb.shape
    t = tri.shape[0]
    qblk = lambda width: pl.BlockSpec((1, t, width), lambda i, j: (i, j, 0))
    full = lambda width: pl.BlockSpec((1, s, width), lambda i, j: (i, 0, 0))
    return pl.pallas_call(
        _attn_prompt_kernel,
        out_shape=jax.ShapeDtypeStruct((b, s, SB_W + MLA_W), BF16),
        grid=(b, s // t),
        in_specs=[qblk(SB_W), qblk(MLA_W), qblk(PE_W), full(SB_W), full(SB_W), full(MLA_W), full(LANES),
                  full(MLA_W), pl.BlockSpec((t, t), lambda i, j: (0, 0))],
        out_specs=qblk(SB_W + MLA_W),
        scratch_shapes=[pltpu.VMEM((SB_HEADS, t, LANES), F32), pltpu.VMEM((SB_HEADS, t, LANES), F32),
                        pltpu.VMEM((SB_HEADS, t, 1), F32), pltpu.VMEM((SB_HEADS, t, 1), F32),
                        pltpu.VMEM((SB_HEADS, t, 1), F32)],
        name="attn_prompt",
        compiler_params=pltpu.CompilerParams(dimension_semantics=("arbitrary", "arbitrary"),
                                             vmem_limit_bytes=VMEM_LIMIT),
    )(qsb, qn, qpe, ksb, vsb, kn, kpe, vm, tri)


def _attn_sample_kernel(qsb_ref, qn_ref, qpe_ref, ksb_ref, vsb_ref, kn_ref, kpe_ref, vm_ref,
                        cksb_ref, cvsb_ref, ckn_ref, ckpe_ref, cvm_ref, tri_ref,
                        o_ref, acc_sb, acc_m, c_s, m_s, l_s, *, past_len):
    tq = qsb_ref.shape[1]
    j = pl.program_id(1)
    q_ops = [_head_operands(qsb_ref.at[0], qn_ref.at[0], qpe_ref.at[0], h) for h in range(SB_HEADS)]
    tri = tri_ref[...]

    @pl.when(j == 0)
    def _():
        sb_mask, mla_mask = _diag_masks(tq, tq, past_len, past_len)
        _attend_block(q_ops, ksb_ref[0], vsb_ref[0], kn_ref[0], kpe_ref[0], vm_ref[0], tri[:tq, :tq],
                      sb_mask, mla_mask, True, acc_sb, acc_m, c_s, m_s, l_s)

    _attend_block(q_ops, cksb_ref[0].astype(BF16), cvsb_ref[0].astype(BF16), ckn_ref[0], ckpe_ref[0],
                  cvm_ref[0], tri, None, None, False, acc_sb, acc_m, c_s, m_s, l_s)

    @pl.when(j == pl.num_programs(1) - 1)
    def _():
        _write_out(o_ref.at[0], acc_sb, acc_m, l_s)


def _attn_sample(qsb, qn, qpe, ksb, vsb, kn, kpe, vm, cksb, cvsb, ckn, ckpe, cvm, tri):
    b, tq, _ = qsb.shape
    past_len = cksb.shape[1]
    tk = tri.shape[0]
    nkb = past_len // tk
    new = lambda width: pl.BlockSpec((1, tq, width), lambda i, j: (i, 0, 0))
    cache = lambda width: pl.BlockSpec((1, tk, width), lambda i, j: (i, nkb - 1 - j, 0))
    return pl.pallas_call(
        functools.partial(_attn_sample_kernel, past_len=past_len),
        out_shape=jax.ShapeDtypeStruct((b, tq, SB_W + MLA_W), BF16),
        grid=(b, nkb),
        in_specs=[new(SB_W), new(MLA_W), new(PE_W), new(SB_W), new(SB_W), new(MLA_W), new(LANES), new(MLA_W),
                  cache(SB_W), cache(SB_W), cache(MLA_W), cache(LANES), cache(MLA_W),
                  pl.BlockSpec((tk, tk), lambda i, j: (0, 0))],
        out_specs=new(SB_W + MLA_W),
        scratch_shapes=[pltpu.VMEM((SB_HEADS, tq, LANES), F32), pltpu.VMEM((SB_HEADS, tq, LANES), F32),
                        pltpu.VMEM((SB_HEADS, tq, 1), F32), pltpu.VMEM((SB_HEADS, tq, 1), F32),
                        pltpu.VMEM((SB_HEADS, tq, 1), F32)],
        name="attn_sample",
        compiler_params=pltpu.CompilerParams(dimension_semantics=("arbitrary", "arbitrary"),
                                             vmem_limit_bytes=VMEM_LIMIT),
    )(qsb, qn, qpe, ksb, vsb, kn, kpe, vm, cksb, cvsb, ckn, ckpe, cvm, tri)


def _post_kernel(x_ref, o_ref, mod_ref, wout_ref, gffn_ref, wr_ref, br_ref, wg_ref, wu_ref, wd_ref,
                 out_ref, x1_s, h2_s, gates_s, y_s):
    nb, ts, d = x_ref.shape
    r = nb * ts
    e = pl.program_id(2)

    @pl.when(e == 0)
    def _():
        gt_a = mod_ref[:, 2:3, :]
        sh_m = mod_ref[:, 3:4, :]
        sc_m = mod_ref[:, 4:5, :]
        att = _dot(o_ref[...].reshape(r, o_ref.shape[-1]), wout_ref[...])
        x1 = x_ref[...] + gt_a * att.reshape(nb, ts, d)
        x1_s[...] = x1
        h2 = (_rms(x1, gffn_ref[...]) * (1.0 + sc_m) + sh_m).reshape(r, d).astype(BF16)
        h2_s[...] = h2
        logits = _dot(h2, wr_ref[...]) + br_ref[...]
        lane_i = lax.broadcasted_iota(jnp.int32, (r, LANES), 1)
        lane = lane_i.astype(F32)
        group_of_lane = (lane_i >> (EXP_PER_GROUP.bit_length() - 1)).astype(F32)
        neg_inf = jnp.float32(-jnp.inf)
        no_lane = jnp.float32(LANES)
        gl = jnp.where(lane_i < N_GROUPS, logits[:, LANES:], neg_inf)
        gmax = jnp.max(gl, axis=-1, keepdims=True)
        gsel = jnp.min(jnp.where(gl == gmax, lane, no_lane), axis=-1, keepdims=True)
        p_g = 1.0 / jnp.sum(jnp.exp(gl - gmax), axis=-1, keepdims=True)
        in_group = group_of_lane == gsel
        em = jnp.where(in_group, logits[:, :LANES], neg_inf)
        v1 = jnp.max(em, axis=-1, keepdims=True)
        i1 = jnp.min(jnp.where(em == v1, lane, no_lane), axis=-1, keepdims=True)
        em2 = jnp.where(lane == i1, neg_inf, em)
        v2 = jnp.max(em2, axis=-1, keepdims=True)
        i2 = jnp.min(jnp.where(em2 == v2, lane, no_lane), axis=-1, keepdims=True)
        t = jnp.exp(v2 - v1)
        w1 = p_g / (1.0 + t)
        gates_s[...] = jnp.where(lane == i1, w1, 0.0) + jnp.where(lane == i2, w1 * t, 0.0)
        y_s[...] = jnp.zeros_like(y_s)

    h2 = h2_s[...]
    a = _dot(h2, wg_ref[0])
    bb = _dot(h2, wu_ref[0])
    lane = lax.broadcasted_iota(jnp.int32, (r, LANES), 1)
    g_e = jnp.sum(jnp.where(lane == e, gates_s[...], 0.0), axis=-1, keepdims=True)
    act = (a / (1.0 + jnp.exp(-a))) * bb * g_e
    y_s[...] += _dot(act.astype(BF16), wd_ref[0])

    @pl.when(e == pl.num_programs(2) - 1)
    def _():
        gt_m = mod_ref[:, 5:6, :]
        out_ref[...] = x1_s[...] + gt_m * y_s[...].reshape(nb, ts, d)


def _post(x, o, mod, w, nb, ts):
    b, s, d = x.shape
    r = nb * ts
    tok = lambda width: pl.BlockSpec((nb, ts, width), lambda i, j, e: (i, j, 0))
    const = lambda a: pl.BlockSpec(a.shape, lambda i, j, e: (0,) * a.ndim)
    per_e = lambda a: pl.BlockSpec((1,) + a.shape[1:], lambda i, j, e: (e, 0, 0))
    return pl.pallas_call(
        _post_kernel,
        out_shape=jax.ShapeDtypeStruct((b, s, d), F32),
        grid=(b // nb, s // ts, N_EXPERTS),
        in_specs=[tok(d), tok(o.shape[-1]), pl.BlockSpec((nb, 6, d), lambda i, j, e: (i, 0, 0)),
                  const(w["w_out"]), const(w["g_ffn"]), const(w["w_r"]), const(w["b_r"]),
                  per_e(w["w_gate"]), per_e(w["w_up"]), per_e(w["w_down"])],
        out_specs=tok(d),
        scratch_shapes=[pltpu.VMEM((nb, ts, d), F32), pltpu.VMEM((r, d), BF16), pltpu.VMEM((r, LANES), F32),
                        pltpu.VMEM((r, d), F32)],
        name="post",
        compiler_params=pltpu.CompilerParams(dimension_semantics=("arbitrary", "arbitrary", "arbitrary"),
                                             vmem_limit_bytes=VMEM_LIMIT),
    )(x, o, mod, w["w_out"], w["g_ffn"], w["w_r"], w["b_r"], w["w_gate"], w["w_up"], w["w_down"])


def _block_mean(width, seg):
    idx = jnp.arange(width) // seg
    return jnp.where(idx[:, None] == idx[None, :], 1.0 / seg, 0.0).astype(BF16)


def _layer_weights(l, g_mix, w_in, g_sb_q, g_sb_k, g_qa, w_qb, g_kva, w_kvb, g_mq_nope, g_mk_nope, g_mq_pe,
                   g_mk_pe, w_out, g_ffn, w_rg, b_rg, w_re, b_re, w_gate, w_up, w_down):
    d = w_in.shape[1]
    row = lambda g, reps: jnp.tile(g[l], reps).reshape(1, -1)
    kpe_cols = jnp.tile(w_in[l][:, C_KPE:C_KPE + MLA_ROPE], (1, KPE_TILE))
    w_qb3 = w_qb[l].reshape(Q_RANK, MLA_HEADS, MLA_NOPE + MLA_ROPE)
    w_kvb3 = w_kvb[l].reshape(KV_RANK, MLA_HEADS, MLA_NOPE + MLA_V)
    pad = jnp.zeros((d, LANES), F32)
    w_r = jnp.concatenate([pad.at[:, :N_EXPERTS].set(w_re[l]), pad.at[:, :N_GROUPS].set(w_rg[l])], axis=1)
    zrow = jnp.zeros((LANES,), F32)
    b_r = jnp.concatenate([zrow.at[:N_EXPERTS].set(b_re[l]), zrow.at[:N_GROUPS].set(b_rg[l])]).reshape(1, -1)
    rep = (jnp.arange(MLA_ROPE)[:, None] == (jnp.arange(LANES)[None, :] % MLA_ROPE)).astype(BF16)
    return dict(
        g_mix=row(g_mix, 1),
        w_in=jnp.concatenate([w_in[l][:, :C_KPE], kpe_cols], axis=1).astype(BF16),
        g_sb_q=row(g_sb_q, SB_HEADS), g_sb_k=row(g_sb_k, SB_HEADS), g_qa=row(g_qa, 1),
        w_qn=w_qb3[:, :, :MLA_NOPE].reshape(Q_RANK, MLA_W).astype(BF16),
        w_qp=w_qb3[:, :, MLA_NOPE:].reshape(Q_RANK, PE_W).astype(BF16),
        g_kva=row(g_kva, 1),
        w_kk=w_kvb3[:, :, :MLA_NOPE].reshape(KV_RANK, MLA_W).astype(BF16),
        w_kv=w_kvb3[:, :, MLA_NOPE:].reshape(KV_RANK, MLA_W).astype(BF16),
        g_mq_nope=row(g_mq_nope, MLA_HEADS), g_mk_nope=row(g_mk_nope, MLA_HEADS),
        g_mq_pe=row(g_mq_pe, MLA_HEADS), g_mk_pe=row(g_mk_pe, KPE_TILE),
        m64=_block_mean(SB_W, SB_DIM), m32=_block_mean(PE_W, MLA_ROPE), rep=rep,
        w_out=w_out[l].astype(BF16), g_ffn=row(g_ffn, 1), w_r=w_r.astype(BF16), b_r=b_r,
        w_gate=w_gate[l].astype(BF16), w_up=w_up[l].astype(BF16), w_down=w_down[l].astype(BF16),
    )


def _rope_tables(pos):
    inv = ROPE_THETA ** (-jnp.arange(0, MLA_ROPE, 2, dtype=F32) / MLA_ROPE)
    ang = pos.astype(F32)[:, None] * inv[None, :]
    cos, sin = jnp.cos(ang), jnp.sin(ang)
    cos = jnp.tile(jnp.concatenate([cos, cos], axis=-1), (1, MLA_HEADS))
    sin = jnp.tile(jnp.concatenate([-sin, sin], axis=-1), (1, MLA_HEADS))
    return cos[None], sin[None]


def _largest_tile(n, cap):
    t = min(n, cap)
    while n % t:
        t -= 8
    return t


def kernel(x_prompt, x_sample, cache_sb_k, cache_sb_v, cache_mla_ckv, cache_mla_kpe, c_prompt, c_sample, w_ada, b_ada, g_mix, w_in, g_sb_q, g_sb_k, g_qa, w_qb, g_kva, w_kvb, g_mq_nope, g_mk_nope, g_mq_pe, g_mk_pe, w_out, g_ffn, w_rg, b_rg, w_re, b_re, w_gate, w_up, w_down):
    bp, sp, d = x_prompt.shape
    bs, ss, _ = x_sample.shape
    depth = w_ada.shape[0]
    past_len = cache_sb_k.shape[2]
    t_attn = 256
    assert sp % t_attn == 0 and past_len % t_attn == 0 and ss <= t_attn and ss % 8 == 0

    mod = _ada(jnp.concatenate([c_prompt, c_sample], axis=0), w_ada, b_ada).reshape(depth, bp + bs, 6, d)
    cos_p, sin_p = _rope_tables(jnp.arange(sp, dtype=jnp.int32))
    cos_s, sin_s = _rope_tables(past_len + jnp.arange(ss, dtype=jnp.int32))
    tri = (jnp.arange(t_attn)[:, None] > jnp.arange(t_attn)[None, :]).astype(BF16)
    cache_k = cache_sb_k.reshape(depth, bs, past_len, SB_W)
    cache_v = cache_sb_v.reshape(depth, bs, past_len, SB_W)

    ts_p = _largest_tile(sp, 512)
    ts_post = _largest_tile(sp, 1024)
    yp, ys = x_prompt, x_sample
    st_p, st_s = [], []
    for l in range(depth):
        w = _layer_weights(l, g_mix, w_in, g_sb_q, g_sb_k, g_qa, w_qb, g_kva, w_kvb, g_mq_nope, g_mk_nope,
                           g_mq_pe, g_mk_pe, w_out, g_ffn, w_rg, b_rg, w_re, b_re, w_gate, w_up, w_down)
        mod_p, mod_s = mod[l, :bp], mod[l, bp:]

        ksb_f, vsb_f, ckv_f, kpe_f, qsb, ksb, vsb, qn, qpe, kn, vm, kpe = _pre(yp, mod_p, cos_p, sin_p, w, 1, ts_p)
        o = _attn_prompt(qsb, qn, qpe, ksb, vsb, kn, kpe, vm, tri)
        yp = _post(yp, o, mod_p, w, 1, ts_post)
        st_p.append((ksb_f, vsb_f, ckv_f, kpe_f))

        ksb_f, vsb_f, ckv_f, kpe_f, qsb, ksb, vsb, qn, qpe, kn, vm, kpe = _pre(ys, mod_s, cos_s, sin_s, w, bs, ss)
        ckn, cvm, ckpe = _expand(cache_mla_ckv[l], cache_mla_kpe[l], w, _largest_tile(past_len, 1024))
        o = _attn_sample(qsb, qn, qpe, ksb, vsb, kn, kpe, vm, cache_k[l], cache_v[l], ckn, ckpe, cvm, tri)
        ys = _post(ys, o, mod_s, w, bs, ss)
        st_s.append((ksb_f, vsb_f, ckv_f, kpe_f))

    def stacked(states, i, b, s):
        a = jnp.stack([st[i] for st in states])
        return a.reshape(depth, b, s, SB_HEADS, SB_DIM) if i < 2 else a

    return (yp, ys,
            stacked(st_p, 0, bp, sp), stacked(st_p, 1, bp, sp), stacked(st_p, 2, bp, sp), stacked(st_p, 3, bp, sp),
            stacked(st_s, 0, bs, ss), stacked(st_s, 1, bs, ss), stacked(st_s, 2, bs, ss), stacked(st_s, 3, bs, ss))
```

```python
import functools
import math

import jax
import jax.numpy as jnp
from jax import lax
from jax.experimental import pallas as pl
from jax.experimental.pallas import tpu as pltpu

F32 = jnp.float32
BF16 = jnp.bfloat16

EPS = 1e-6
CHUNK = 64
SB_HEADS = 8
SB_DIM = 64
SB_W = SB_HEADS * SB_DIM
MLA_HEADS = 8
MLA_NOPE = 64
MLA_ROPE = 32
MLA_V = 64
Q_RANK = 384
KV_RANK = 256
ROPE_THETA = 10000.0
MLA_W = MLA_HEADS * MLA_V
N_GROUPS = 4
EXP_PER_GROUP = 8
N_EXPERTS = N_GROUPS * EXP_PER_GROUP
D_EXPERT = 256

LANES = 128
PAIR_W = 2 * SB_DIM
N_PAIRS = SB_HEADS // 2
KPE_TILE = LANES // MLA_ROPE
PE_W = MLA_HEADS * MLA_ROPE
C_QSB, C_KSB, C_VSB = 0, SB_W, 2 * SB_W
C_QLAT = 3 * SB_W
C_KVLAT = C_QLAT + Q_RANK
C_KPE = C_KVLAT + KV_RANK
IN_COLS_PAD = C_KPE + LANES
MLA_SCALE = 1.0 / math.sqrt(MLA_NOPE + MLA_ROPE)
NEG_BIG = -1e30
VMEM_LIMIT = 56 * 1024 * 1024


def _nt_dot(a, b):
    return lax.dot_general(a, b, (((1,), (1,)), ((), ())), preferred_element_type=F32)


def _dot(a, b):
    return jnp.dot(a, b, preferred_element_type=F32)


def _rms(x, g):
    ms = jnp.mean(x * x, axis=-1, keepdims=True)
    return x * lax.rsqrt(ms + EPS) * g


def _seg_rms(x, seg_mean, g):
    ms = _dot((x * x).astype(BF16), seg_mean)
    return x * lax.rsqrt(ms + EPS) * g


def _rot_half(x, lane_in_group_lt_half):
    w = x.shape[-1]
    fwd = pltpu.roll(x, w - MLA_ROPE // 2, axis=1)
    bwd = pltpu.roll(x, MLA_ROPE // 2, axis=1)
    return jnp.where(lane_in_group_lt_half, fwd, bwd)


def _ada_kernel(c_ref, w_ref, b_ref, o_ref):
    c = c_ref[...]
    s = c / (1.0 + jnp.exp(-c))
    o_ref[0] = _dot(s.astype(BF16), w_ref[0].astype(BF16)) + b_ref[0]


def _ada(c_all, w_ada, b_ada):
    depth, d, n = w_ada.shape
    nb = c_all.shape[0]
    tn = d
    return pl.pallas_call(
        _ada_kernel,
        out_shape=jax.ShapeDtypeStruct((depth, nb, n), F32),
        grid=(depth, n // tn),
        in_specs=[
            pl.BlockSpec((nb, d), lambda l, j: (0, 0)),
            pl.BlockSpec((1, d, tn), lambda l, j: (l, 0, j)),
            pl.BlockSpec((1, 1, tn), lambda l, j: (l, 0, j)),
        ],
        out_specs=pl.BlockSpec((1, nb, tn), lambda l, j: (l, 0, j)),
        name="ada",
        compiler_params=pltpu.CompilerParams(dimension_semantics=("arbitrary", "arbitrary")),
    )(c_all, w_ada, b_ada.reshape(depth, 1, n))


def _pre_kernel(x_ref, mod_ref, cos_ref, sin_ref, gmix_ref, win_ref, gsq_ref, gsk_ref, gqa_ref,
                wqn_ref, wqp_ref, gkva_ref, wkk_ref, wkv_ref, gqn_ref, gkn_ref, gqpe_ref, gkpe_ref,
                m64_ref, m32_ref,
                ksb_o, vsb_o, ckv_o, kpe_o,
                qsb_b, ksb_b, vsb_b, qn_b, qpe_b, kn_b, vm_b, kpe_b):
    nb, ts, d = x_ref.shape
    r = nb * ts
    x = x_ref[...]
    sh = mod_ref[:, 0:1, :]
    sc = mod_ref[:, 1:2, :]
    h = _rms(x, gmix_ref[...]) * (1.0 + sc) + sh
    u = _dot(h.reshape(r, d).astype(BF16), win_ref[...])

    m64 = m64_ref[...]
    q_sb = _seg_rms(u[:, C_QSB:C_QSB + SB_W], m64, gsq_ref[...])
    k_sb = _seg_rms(u[:, C_KSB:C_KSB + SB_W], m64, gsk_ref[...])
    v_sb = u[:, C_VSB:C_VSB + SB_W]
    ksb_o[0] = k_sb.reshape(nb, ts, SB_W)
    vsb_o[0] = v_sb.reshape(nb, ts, SB_W)
    qsb_b[...] = (q_sb * (1.0 / math.sqrt(SB_DIM))).astype(BF16).reshape(nb, ts, SB_W)
    ksb_b[...] = k_sb.astype(BF16).reshape(nb, ts, SB_W)
    vsb_b[...] = v_sb.astype(BF16).reshape(nb, ts, SB_W)

    cos = cos_ref[...]
    sin = sin_ref[...]
    lane = lax.broadcasted_iota(jnp.int32, (1, PE_W), 1)
    first_half = (lane & (MLA_ROPE - 1)) < (MLA_ROPE // 2)

    q_lat = _rms(u[:, C_QLAT:C_QLAT + Q_RANK], gqa_ref[...]).astype(BF16)
    qn = _seg_rms(_dot(q_lat, wqn_ref[...]), m64, gqn_ref[...])
    qn_b[...] = (qn * MLA_SCALE).astype(BF16).reshape(nb, ts, MLA_W)
    qp = _seg_rms(_dot(q_lat, wqp_ref[...]), m32_ref[...], gqpe_ref[...])
    qp_rot = _rot_half(qp, first_half)
    qp = qp.reshape(nb, ts, PE_W) * cos + qp_rot.reshape(nb, ts, PE_W) * sin
    qpe_b[...] = (qp * MLA_SCALE).astype(BF16)

    ckv = _rms(u[:, C_KVLAT:C_KVLAT + KV_RANK], gkva_ref[...])
    ckv_o[0] = ckv.reshape(nb, ts, KV_RANK)
    ckv16 = ckv.astype(BF16)
    kn = _seg_rms(_dot(ckv16, wkk_ref[...]), m64, gkn_ref[...])
    kn_b[...] = kn.astype(BF16).reshape(nb, ts, MLA_W)
    vm_b[...] = _dot(ckv16, wkv_ref[...]).astype(BF16).reshape(nb, ts, MLA_W)

    kp = _rms(u[:, C_KPE:C_KPE + LANES], gkpe_ref[...])
    kp_rot = _rot_half(kp, first_half[:, :LANES])
    kp = kp.reshape(nb, ts, LANES) * cos[:, :, :LANES] + kp_rot.reshape(nb, ts, LANES) * sin[:, :, :LANES]
    kpe_o[0] = kp[:, :, :MLA_ROPE]
    kpe_b[...] = kp.astype(BF16)


def _pre_kernel_aliased(*refs):
    _pre_kernel(*refs[N_STATE:])


N_STATE = 4


def _pre(x, mod, cos, sin, w, nb, ts, layer, depth, state):
    b, s, d = x.shape
    grid = (b // nb, s // ts)
    tok = lambda width: pl.BlockSpec((nb, ts, width), lambda i, j: (i, j, 0))
    slab = lambda width: pl.BlockSpec((1, nb, ts, width), lambda i, j: (layer, i, j, 0))
    const = lambda a: pl.BlockSpec(a.shape, lambda i, j: (0,) * a.ndim)
    weights = [w["g_mix"], w["w_in"], w["g_sb_q"], w["g_sb_k"], w["g_qa"], w["w_qn"], w["w_qp"], w["g_kva"],
               w["w_kk"], w["w_kv"], w["g_mq_nope"], w["g_mk_nope"], w["g_mq_pe"], w["g_mk_pe"],
               w["m64"], w["m32"]]
    state_widths = (SB_W, SB_W, KV_RANK, MLA_ROPE)
    bf16_widths = (SB_W, SB_W, SB_W, MLA_W, PE_W, MLA_W, MLA_W, LANES)
    out_shape = tuple(jax.ShapeDtypeStruct((depth, b, s, wd), F32) for wd in state_widths) + \
        tuple(jax.ShapeDtypeStruct((b, s, wd), BF16) for wd in bf16_widths)
    in_specs = [tok(d),
                pl.BlockSpec((nb, 6, d), lambda i, j: (i, 0, 0)),
                pl.BlockSpec((1, ts, PE_W), lambda i, j: (0, j, 0)),
                pl.BlockSpec((1, ts, PE_W), lambda i, j: (0, j, 0))] + [const(a) for a in weights]
    args = (x, mod, cos, sin, *weights)
    body, aliases = _pre_kernel, {}
    if state is not None:
        body, aliases = _pre_kernel_aliased, {i: i for i in range(N_STATE)}
        in_specs = [pl.BlockSpec(memory_space=pl.ANY)] * N_STATE + in_specs
        args = (*state, *args)
    outs = pl.pallas_call(
        body,
        out_shape=out_shape,
        grid=grid,
        in_specs=in_specs,
        out_specs=tuple(slab(wd) for wd in state_widths) + tuple(tok(wd) for wd in bf16_widths),
        input_output_aliases=aliases,
        name="pre",
        compiler_params=pltpu.CompilerParams(dimension_semantics=("arbitrary", "arbitrary"),
                                             vmem_limit_bytes=VMEM_LIMIT),
    )(*args)
    return outs[:N_STATE], outs[N_STATE:]


def _expand_kernel(ckv_ref, kpe_ref, wkk_ref, wkv_ref, gkn_ref, m64_ref, rep_ref, kn_b, vm_b, kpe_b):
    ckv16 = ckv_ref[0, 0].astype(BF16)
    kn = _seg_rms(_dot(ckv16, wkk_ref[...]), m64_ref[...], gkn_ref[...])
    kn_b[0] = kn.astype(BF16)
    vm_b[0] = _dot(ckv16, wkv_ref[...]).astype(BF16)
    kpe_b[0] = _dot(kpe_ref[0, 0].astype(BF16), rep_ref[...]).astype(BF16)


def _expand(layer, ckv, kpe, w, tp):
    _, b, p, _ = ckv.shape
    const = lambda a: pl.BlockSpec(a.shape, lambda i, j: (0,) * a.ndim)
    weights = [w["w_kk"], w["w_kv"], w["g_mk_nope"], w["m64"], w["rep"]]
    return pl.pallas_call(
        _expand_kernel,
        out_shape=(jax.ShapeDtypeStruct((b, p, MLA_W), BF16), jax.ShapeDtypeStruct((b, p, MLA_W), BF16),
                   jax.ShapeDtypeStruct((b, p, LANES), BF16)),
        grid=(b, p // tp),
        in_specs=[pl.BlockSpec((1, 1, tp, KV_RANK), lambda i, j: (layer, i, j, 0)),
                  pl.BlockSpec((1, 1, tp, MLA_ROPE), lambda i, j: (layer, i, j, 0))] + [const(a) for a in weights],
        out_specs=(pl.BlockSpec((1, tp, MLA_W), lambda i, j: (i, j, 0)),
                   pl.BlockSpec((1, tp, MLA_W), lambda i, j: (i, j, 0)),
                   pl.BlockSpec((1, tp, LANES), lambda i, j: (i, j, 0))),
        name="expand",
        compiler_params=pltpu.CompilerParams(dimension_semantics=("arbitrary", "arbitrary"),
                                             vmem_limit_bytes=VMEM_LIMIT),
    )(ckv, kpe, *weights)


def _stick_logs(z, mask):
    sign_bit = jnp.uint32(0x80000000)
    neg_abs = lax.bitcast_convert_type(lax.bitcast_convert_type(z, jnp.uint32) | sign_bit, F32)
    log_beta = jnp.minimum(z, 0.0) - jnp.log(1.0 + jnp.exp(neg_abs))
    log_1m = log_beta - z
    if mask is not None:
        log_1m = jnp.where(mask, log_1m, 0.0)
    return log_beta, log_1m


def _head_operands(qsb_ref, qn_ref, qpe_ref, h):
    p, e = divmod(h, 2)
    lane = lax.broadcasted_iota(jnp.int32, (1, LANES), 1)
    half = (lane >= SB_DIM) if e == 1 else (lane < SB_DIM)
    sl = slice(PAIR_W * p, PAIR_W * (p + 1))
    q_sb = jnp.where(half, qsb_ref[:, sl], jnp.zeros((), BF16))
    q_n = jnp.where(half, qn_ref[:, sl], jnp.zeros((), BF16))
    slab, pos = divmod(h, KPE_TILE)
    pe_lanes = (lane >= MLA_ROPE * pos) & (lane < MLA_ROPE * (pos + 1))
    q_pe = jnp.where(pe_lanes, qpe_ref[:, LANES * slab:LANES * (slab + 1)], jnp.zeros((), BF16))
    return q_sb, jnp.concatenate([q_n, q_pe], axis=-1)


def _attend_block(q_ops, k_sb, v_sb, k_n, k_pe, v_m, tri, sb_mask, mla_mask, first,
                  acc_sb, acc_m, c_s, m_s, l_s):
    heads = range(SB_HEADS)
    slab = lambda a, h: a[:, PAIR_W * (h // 2):PAIR_W * (h // 2 + 1)]
    kcat = [jnp.concatenate([k_n[:, PAIR_W * p:PAIR_W * (p + 1)], k_pe], axis=-1) for p in range(N_PAIRS)]

    z_sb = [_nt_dot(q_ops[h][0], slab(k_sb, h)) for h in heads]
    z_m = [_nt_dot(q_ops[h][1], kcat[h // 2]) for h in heads]

    log_w, suffix = [], []
    for h in heads:
        log_beta, log_1m = _stick_logs(z_sb[h], sb_mask)
        suffix.append(_dot(log_1m.astype(BF16), tri))
        row_sum = jnp.sum(log_1m, axis=-1, keepdims=True)
        if first:
            log_w.append(log_beta)
            c_s[h] = row_sum
        else:
            carry = c_s[h]
            log_w.append(log_beta + carry)
            c_s[h] = carry + row_sum

    for h in heads:
        z = z_m[h]
        if mla_mask is not None:
            z = jnp.where(mla_mask, z, NEG_BIG)
        zmax = jnp.max(z, axis=-1, keepdims=True)
        if first:
            prob = jnp.exp(z - zmax)
            m_s[h] = zmax
            l_s[h] = jnp.sum(prob, axis=-1, keepdims=True)
            acc_m[h] = _dot(prob.astype(BF16), slab(v_m, h))
        else:
            m_old = m_s[h]
            m_new = jnp.maximum(m_old, zmax)
            prob = jnp.exp(z - m_new)
            alpha = jnp.exp(m_old - m_new)
            m_s[h] = m_new
            l_s[h] = alpha * l_s[h] + jnp.sum(prob, axis=-1, keepdims=True)
            acc_m[h] = alpha * acc_m[h] + _dot(prob.astype(BF16), slab(v_m, h))

    for h in heads:
        wgt = jnp.exp(log_w[h] + suffix[h])
        if sb_mask is not None:
            wgt = jnp.where(sb_mask, wgt, 0.0)
        pv = _dot(wgt.astype(BF16), slab(v_sb, h))
        if first:
            acc_sb[h] = pv
        else:
            acc_sb[h] += pv


def _write_out(o_ref, acc_sb, acc_m, l_s):
    lane = lax.broadcasted_iota(jnp.int32, (1, LANES), 1)
    lo_half = lane < SB_DIM
    for p in range(N_PAIRS):
        h0, h1 = 2 * p, 2 * p + 1
        o_ref[:, PAIR_W * p:PAIR_W * (p + 1)] = jnp.where(lo_half, acc_sb[h0], acc_sb[h1]).astype(o_ref.dtype)
        mla = jnp.where(lo_half, acc_m[h0] / l_s[h0], acc_m[h1] / l_s[h1])
        o_ref[:, SB_W + PAIR_W * p:SB_W + PAIR_W * (p + 1)] = mla.astype(o_ref.dtype)


def _diag_masks(tq, tk, q_pos0, k_pos0):
    qpos = q_pos0 + lax.broadcasted_iota(jnp.int32, (tq, tk), 0)
    kpos = k_pos0 + lax.broadcasted_iota(jnp.int32, (tq, tk), 1)
    shift = CHUNK.bit_length() - 1
    return kpos < qpos, (kpos >> shift) <= (qpos >> shift)


def _attn_prompt_kernel(qsb_ref, qn_ref, qpe_ref, ksb_ref, vsb_ref, kn_ref, kpe_ref, vm_ref, tri_ref,
                        o_ref, acc_sb, acc_m, c_s, m_s, l_s):
    t = qsb_ref.shape[1]
    qi = pl.program_id(1)
    q_ops = [_head_operands(qsb_ref.at[0], qn_ref.at[0], qpe_ref.at[0], h) for h in range(SB_HEADS)]
    tri = tri_ref[...]

    def key_block(start):
        rows = pl.ds(start, t)
        return (ksb_ref[0, rows, :], vsb_ref[0, rows, :], kn_ref[0, rows, :], kpe_ref[0, rows, :],
                vm_ref[0, rows, :])

    sb_mask, mla_mask = _diag_masks(t, t, 0, 0)
    k_sb, v_sb, k_n, k_pe, v_m = key_block(pl.multiple_of(qi * t, t))
    _attend_block(q_ops, k_sb, v_sb, k_n, k_pe, v_m, tri, sb_mask, mla_mask, True,
                  acc_sb, acc_m, c_s, m_s, l_s)

    def body(j, _):
        kb = qi - 1 - j
        k_sb, v_sb, k_n, k_pe, v_m = key_block(pl.multiple_of(kb * t, t))
        _attend_block(q_ops, k_sb, v_sb, k_n, k_pe, v_m, tri, None, None, False,
                      acc_sb, acc_m, c_s, m_s, l_s)
        return 0

    lax.fori_loop(0, qi, body, 0)
    _write_out(o_ref.at[0], acc_sb, acc_m, l_s)


def _attn_prompt(qsb, qn, qpe, ksb, vsb, kn, kpe, vm, tri):
    b, s, _ = qsb.shape
    t = tri.shape[0]
    qblk = lambda width: pl.BlockSpec((1, t, width), lambda i, j: (i, j, 0))
    full = lambda width: pl.BlockSpec((1, s, width), lambda i, j: (i, 0, 0))
    return pl.pallas_call(
        _attn_prompt_kernel,
        out_shape=jax.ShapeDtypeStruct((b, s, SB_W + MLA_W), BF16),
        grid=(b, s // t),
        in_specs=[qblk(SB_W), qblk(MLA_W), qblk(PE_W), full(SB_W), full(SB_W), full(MLA_W), full(LANES),
                  full(MLA_W), pl.BlockSpec((t, t), lambda i, j: (0, 0))],
        out_specs=qblk(SB_W + MLA_W),
        scratch_shapes=[pltpu.VMEM((SB_HEADS, t, LANES), F32), pltpu.VMEM((SB_HEADS, t, LANES), F32),
                        pltpu.VMEM((SB_HEADS, t, 1), F32), pltpu.VMEM((SB_HEADS, t, 1), F32),
                        pltpu.VMEM((SB_HEADS, t, 1), F32)],
        name="attn_prompt",
        compiler_params=pltpu.CompilerParams(dimension_semantics=("arbitrary", "arbitrary"),
                                             vmem_limit_bytes=VMEM_LIMIT),
    )(qsb, qn, qpe, ksb, vsb, kn, kpe, vm, tri)


def _tn_dot(a, b):
    return lax.dot_general(a, b, (((0,), (0,)), ((), ())), preferred_element_type=F32)


def _sample_block(k_sb, v_sb, k_cat, v_m, tri_u, sb_mask, mla_mask, first,
                  qbd_sb, qbd_m, acc_sb, acc_m, c_s, m_s, l_s):
    z = _dot(k_sb, qbd_sb[...])
    log_beta, log_1m = _stick_logs(z, sb_mask)
    suffix = _dot(tri_u, log_1m.astype(BF16))
    col_sum = jnp.sum(log_1m, axis=0, keepdims=True)
    if first:
        wgt = jnp.exp(log_beta + suffix)
        c_s[...] = col_sum
    else:
        carry = c_s[...]
        wgt = jnp.exp(log_beta + suffix + carry)
        c_s[...] = carry + col_sum
    if sb_mask is not None:
        wgt = jnp.where(sb_mask, wgt, 0.0)
    pv = _tn_dot(v_sb, wgt.astype(BF16))
    if first:
        acc_sb[...] = pv
    else:
        acc_sb[...] += pv

    z = _dot(k_cat, qbd_m[...])
    if mla_mask is not None:
        z = jnp.where(mla_mask, z, NEG_BIG)
    zmax = jnp.max(z, axis=0, keepdims=True)
    if first:
        prob = jnp.exp(z - zmax)
        m_s[...] = zmax
        l_s[...] = jnp.sum(prob, axis=0, keepdims=True)
        acc_m[...] = _tn_dot(v_m, prob.astype(BF16))
    else:
        m_old = m_s[...]
        m_new = jnp.maximum(m_old, zmax)
        prob = jnp.exp(z - m_new)
        alpha = jnp.exp(m_old - m_new)
        m_s[...] = m_new
        l_s[...] = alpha * l_s[...] + jnp.sum(prob, axis=0, keepdims=True)
        acc_m[...] = alpha * acc_m[...] + _tn_dot(v_m, prob.astype(BF16))


def _attn_sample_kernel(qsb_ref, qn_ref, qpe_ref, ksb_ref, vsb_ref, kn_ref, kpe_ref, vm_ref,
                        cksb_ref, cvsb_ref, ckn_ref, ckpe_ref, cvm_ref, tri_ref, sel_ref,
                        o_ref, qbd_sb, qbd_m, acc_sb, acc_m, c_s, m_s, l_s, *, past_len):
    tq = qsb_ref.shape[1]
    ncol = SB_HEADS * tq
    q_shift = tq.bit_length() - 1
    j = pl.program_id(1)
    state = (qbd_sb, qbd_m, acc_sb, acc_m, c_s, m_s, l_s)

    @pl.when(j == 0)
    def _():
        sel = sel_ref[...]
        row_head = lax.broadcasted_iota(jnp.int32, (SB_W, ncol), 0) >> (SB_DIM.bit_length() - 1)
        col_head = lax.broadcasted_iota(jnp.int32, (SB_W, ncol), 1) >> q_shift
        on_diag = row_head == col_head
        qbd_sb[...] = jnp.where(on_diag, _tn_dot(qsb_ref[0], sel), 0.0).astype(BF16)
        qbd_m[0:MLA_W, :] = jnp.where(on_diag, _tn_dot(qn_ref[0], sel), 0.0).astype(BF16)
        pe_t = _tn_dot(qpe_ref[0], sel)
        head_of_col = lax.broadcasted_iota(jnp.int32, (MLA_ROPE, ncol), 1) >> q_shift
        pe = jnp.zeros((MLA_ROPE, ncol), F32)
        for h in range(MLA_HEADS):
            pe = pe + jnp.where(head_of_col == h, pe_t[MLA_ROPE * h:MLA_ROPE * (h + 1), :], 0.0)
        qbd_m[MLA_W:MLA_W + LANES, :] = jnp.concatenate(
            [pe, jnp.zeros((LANES - MLA_ROPE, ncol), F32)], axis=0).astype(BF16)

        key_i = lax.broadcasted_iota(jnp.int32, (tq, ncol), 0)
        qry_i = lax.broadcasted_iota(jnp.int32, (tq, ncol), 1) & (tq - 1)
        c_shift = CHUNK.bit_length() - 1
        sb_mask = key_i < qry_i
        mla_mask = ((past_len + key_i) >> c_shift) <= ((past_len + qry_i) >> c_shift)
        k_cat = jnp.concatenate([kn_ref[0], kpe_ref[0]], axis=-1)
        _sample_block(ksb_ref[0], vsb_ref[0], k_cat, vm_ref[0], tri_ref[0:tq, 0:tq], sb_mask, mla_mask, True,
                      *state)

    k_cat = jnp.concatenate([ckn_ref[0], ckpe_ref[0]], axis=-1)
    _sample_block(cksb_ref[0, 0].astype(BF16), cvsb_ref[0, 0].astype(BF16), k_cat, cvm_ref[0], tri_ref[...],
                  None, None, False, *state)

    @pl.when(j == pl.num_programs(1) - 1)
    def _():
        lane_head = lax.broadcasted_iota(jnp.int32, (tq, SB_W), 1) >> (SB_DIM.bit_length() - 1)

        def diag_blocks(acc_t):
            acc = acc_t.T
            out = jnp.zeros((tq, SB_W), F32)
            for h in range(SB_HEADS):
                out = out + jnp.where(lane_head == h, acc[tq * h:tq * (h + 1), :], 0.0)
            return out

        o_ref[0, :, 0:SB_W] = diag_blocks(acc_sb[...]).astype(o_ref.dtype)
        o_ref[0, :, SB_W:SB_W + MLA_W] = diag_blocks(acc_m[...] / l_s[...]).astype(o_ref.dtype)


def _attn_sample(layer, qsb, qn, qpe, ksb, vsb, kn, kpe, vm, cache_k, cache_v, ckn, ckpe, cvm, tri_u, sel):
    b, tq, _ = qsb.shape
    past_len = cache_k.shape[2]
    tk = tri_u.shape[0]
    nkb = past_len // tk
    ncol = SB_HEADS * tq
    new = lambda width: pl.BlockSpec((1, tq, width), lambda i, j: (i, 0, 0))
    rev = lambda width: pl.BlockSpec((1, tk, width), lambda i, j: (i, nkb - 1 - j, 0))
    rev_l = lambda width: pl.BlockSpec((1, 1, tk, width), lambda i, j: (layer, i, nkb - 1 - j, 0))
    const = lambda a: pl.BlockSpec(a.shape, lambda i, j: (0,) * a.ndim)
    return pl.pallas_call(
        functools.partial(_attn_sample_kernel, past_len=past_len),
        out_shape=jax.ShapeDtypeStruct((b, tq, SB_W + MLA_W), BF16),
        grid=(b, nkb),
        in_specs=[new(SB_W), new(MLA_W), new(PE_W), new(SB_W), new(SB_W), new(MLA_W), new(LANES), new(MLA_W),
                  rev_l(SB_W), rev_l(SB_W), rev(MLA_W), rev(LANES), rev(MLA_W), const(tri_u), const(sel)],
        out_specs=new(SB_W + MLA_W),
        scratch_shapes=[pltpu.VMEM((SB_W, ncol), BF16), pltpu.VMEM((MLA_W + LANES, ncol), BF16),
                        pltpu.VMEM((SB_W, ncol), F32), pltpu.VMEM((MLA_W, ncol), F32),
                        pltpu.VMEM((1, ncol), F32), pltpu.VMEM((1, ncol), F32), pltpu.VMEM((1, ncol), F32)],
        name="attn_sample",
        compiler_params=pltpu.CompilerParams(dimension_semantics=("arbitrary", "arbitrary"),
                                             vmem_limit_bytes=VMEM_LIMIT),
    )(qsb, qn, qpe, ksb, vsb, kn, kpe, vm, cache_k, cache_v, ckn, ckpe, cvm, tri_u, sel)


def _post_kernel(x_ref, o_ref, mod_ref, wout_ref, gffn_ref, wr_ref, br_ref, wg_ref, wu_ref, wd_ref,
                 out_ref, x1_s, h2_s, gates_s, y_s):
    nb, ts, d = x_ref.shape
    r = nb * ts
    e = pl.program_id(2)

    @pl.when(e == 0)
    def _():
        gt_a = mod_ref[:, 2:3, :]
        sh_m = mod_ref[:, 3:4, :]
        sc_m = mod_ref[:, 4:5, :]
        att = _dot(o_ref[...].reshape(r, o_ref.shape[-1]), wout_ref[...])
        x1 = x_ref[...] + gt_a * att.reshape(nb, ts, d)
        x1_s[...] = x1
        h2 = (_rms(x1, gffn_ref[...]) * (1.0 + sc_m) + sh_m).reshape(r, d).astype(BF16)
        h2_s[...] = h2
        logits = _dot(h2, wr_ref[...]) + br_ref[...]
        lane_i = lax.broadcasted_iota(jnp.int32, (r, LANES), 1)
        lane = lane_i.astype(F32)
        group_of_lane = (lane_i >> (EXP_PER_GROUP.bit_length() - 1)).astype(F32)
        neg_inf = jnp.float32(-jnp.inf)
        no_lane = jnp.float32(LANES)
        gl = jnp.where(lane_i < N_GROUPS, logits[:, LANES:], neg_inf)
        gmax = jnp.max(gl, axis=-1, keepdims=True)
        gsel = jnp.min(jnp.where(gl == gmax, lane, no_lane), axis=-1, keepdims=True)
        p_g = 1.0 / jnp.sum(jnp.exp(gl - gmax), axis=-1, keepdims=True)
        in_group = group_of_lane == gsel
        em = jnp.where(in_group, logits[:, :LANES], neg_inf)
        v1 = jnp.max(em, axis=-1, keepdims=True)
        i1 = jnp.min(jnp.where(em == v1, lane, no_lane), axis=-1, keepdims=True)
        em2 = jnp.where(lane == i1, neg_inf, em)
        v2 = jnp.max(em2, axis=-1, keepdims=True)
        i2 = jnp.min(jnp.where(em2 == v2, lane, no_lane), axis=-1, keepdims=True)
        t = jnp.exp(v2 - v1)
        w1 = p_g / (1.0 + t)
        gates_s[...] = jnp.where(lane == i1, w1, 0.0) + jnp.where(lane == i2, w1 * t, 0.0)
        y_s[...] = jnp.zeros_like(y_s)

    h2 = h2_s[...]
    a = _dot(h2, wg_ref[0])
    bb = _dot(h2, wu_ref[0])
    lane = lax.broadcasted_iota(jnp.int32, (r, LANES), 1)
    g_e = jnp.sum(jnp.where(lane == e, gates_s[...], 0.0), axis=-1, keepdims=True)
    act = (a / (1.0 + jnp.exp(-a))) * bb * g_e
    y_s[...] += _dot(act.astype(BF16), wd_ref[0])

    @pl.when(e == pl.num_programs(2) - 1)
    def _():
        gt_m = mod_ref[:, 5:6, :]
        out_ref[...] = x1_s[...] + gt_m * y_s[...].reshape(nb, ts, d)


def _post(x, o, mod, w, nb, ts):
    b, s, d = x.shape
    r = nb * ts
    tok = lambda width: pl.BlockSpec((nb, ts, width), lambda i, j, e: (i, j, 0))
    const = lambda a: pl.BlockSpec(a.shape, lambda i, j, e: (0,) * a.ndim)
    per_e = lambda a: pl.BlockSpec((1,) + a.shape[1:], lambda i, j, e: (e, 0, 0))
    return pl.pallas_call(
        _post_kernel,
        out_shape=jax.ShapeDtypeStruct((b, s, d), F32),
        grid=(b // nb, s // ts, N_EXPERTS),
        in_specs=[tok(d), tok(o.shape[-1]), pl.BlockSpec((nb, 6, d), lambda i, j, e: (i, 0, 0)),
                  const(w["w_out"]), const(w["g_ffn"]), const(w["w_r"]), const(w["b_r"]),
                  per_e(w["w_gate"]), per_e(w["w_up"]), per_e(w["w_down"])],
        out_specs=tok(d),
        scratch_shapes=[pltpu.VMEM((nb, ts, d), F32), pltpu.VMEM((r, d), BF16), pltpu.VMEM((r, LANES), F32),
                        pltpu.VMEM((r, d), F32)],
        name="post",
        compiler_params=pltpu.CompilerParams(dimension_semantics=("arbitrary", "arbitrary", "arbitrary"),
                                             vmem_limit_bytes=VMEM_LIMIT),
    )(x, o, mod, w["w_out"], w["g_ffn"], w["w_r"], w["b_r"], w["w_gate"], w["w_up"], w["w_down"])


def _block_mean(width, seg):
    idx = jnp.arange(width) // seg
    return jnp.where(idx[:, None] == idx[None, :], 1.0 / seg, 0.0).astype(BF16)


def _layer_weights(l, g_mix, w_in, g_sb_q, g_sb_k, g_qa, w_qb, g_kva, w_kvb, g_mq_nope, g_mk_nope, g_mq_pe,
                   g_mk_pe, w_out, g_ffn, w_rg, b_rg, w_re, b_re, w_gate, w_up, w_down):
    d = w_in.shape[1]
    row = lambda g, reps: jnp.tile(g[l], reps).reshape(1, -1)
    kpe_cols = jnp.tile(w_in[l][:, C_KPE:C_KPE + MLA_ROPE], (1, KPE_TILE))
    w_qb3 = w_qb[l].reshape(Q_RANK, MLA_HEADS, MLA_NOPE + MLA_ROPE)
    w_kvb3 = w_kvb[l].reshape(KV_RANK, MLA_HEADS, MLA_NOPE + MLA_V)
    pad = jnp.zeros((d, LANES), F32)
    w_r = jnp.concatenate([pad.at[:, :N_EXPERTS].set(w_re[l]), pad.at[:, :N_GROUPS].set(w_rg[l])], axis=1)
    zrow = jnp.zeros((LANES,), F32)
    b_r = jnp.concatenate([zrow.at[:N_EXPERTS].set(b_re[l]), zrow.at[:N_GROUPS].set(b_rg[l])]).reshape(1, -1)
    rep = (jnp.arange(MLA_ROPE)[:, None] == (jnp.arange(LANES)[None, :] % MLA_ROPE)).astype(BF16)
    return dict(
        g_mix=row(g_mix, 1),
        w_in=jnp.concatenate([w_in[l][:, :C_KPE], kpe_cols], axis=1).astype(BF16),
        g_sb_q=row(g_sb_q, SB_HEADS), g_sb_k=row(g_sb_k, SB_HEADS), g_qa=row(g_qa, 1),
        w_qn=w_qb3[:, :, :MLA_NOPE].reshape(Q_RANK, MLA_W).astype(BF16),
        w_qp=w_qb3[:, :, MLA_NOPE:].reshape(Q_RANK, PE_W).astype(BF16),
        g_kva=row(g_kva, 1),
        w_kk=w_kvb3[:, :, :MLA_NOPE].reshape(KV_RANK, MLA_W).astype(BF16),
        w_kv=w_kvb3[:, :, MLA_NOPE:].reshape(KV_RANK, MLA_W).astype(BF16),
        g_mq_nope=row(g_mq_nope, MLA_HEADS), g_mk_nope=row(g_mk_nope, MLA_HEADS),
        g_mq_pe=row(g_mq_pe, MLA_HEADS), g_mk_pe=row(g_mk_pe, KPE_TILE),
        m64=_block_mean(SB_W, SB_DIM), m32=_block_mean(PE_W, MLA_ROPE), rep=rep,
        w_out=w_out[l].astype(BF16), g_ffn=row(g_ffn, 1), w_r=w_r.astype(BF16), b_r=b_r,
        w_gate=w_gate[l].astype(BF16), w_up=w_up[l].astype(BF16), w_down=w_down[l].astype(BF16),
    )


def _rope_tables(pos):
    inv = ROPE_THETA ** (-jnp.arange(0, MLA_ROPE, 2, dtype=F32) / MLA_ROPE)
    ang = pos.astype(F32)[:, None] * inv[None, :]
    cos, sin = jnp.cos(ang), jnp.sin(ang)
    cos = jnp.tile(jnp.concatenate([cos, cos], axis=-1), (1, MLA_HEADS))
    sin = jnp.tile(jnp.concatenate([-sin, sin], axis=-1), (1, MLA_HEADS))
    return cos[None], sin[None]


def _largest_tile(n, cap):
    t = min(n, cap)
    while n % t:
        t -= 8
    return t


def kernel(x_prompt, x_sample, cache_sb_k, cache_sb_v, cache_mla_ckv, cache_mla_kpe, c_prompt, c_sample, w_ada, b_ada, g_mix, w_in, g_sb_q, g_sb_k, g_qa, w_qb, g_kva, w_kvb, g_mq_nope, g_mk_nope, g_mq_pe, g_mk_pe, w_out, g_ffn, w_rg, b_rg, w_re, b_re, w_gate, w_up, w_down):
    bp, sp, d = x_prompt.shape
    bs, ss, _ = x_sample.shape
    depth = w_ada.shape[0]
    past_len = cache_sb_k.shape[2]
    t_attn = 256
    t_cache = 512
    assert sp % t_attn == 0 and past_len % t_cache == 0 and ss % 8 == 0 and ss & (ss - 1) == 0

    mod = _ada(jnp.concatenate([c_prompt, c_sample], axis=0), w_ada, b_ada).reshape(depth, bp + bs, 6, d)
    cos_p, sin_p = _rope_tables(jnp.arange(sp, dtype=jnp.int32))
    cos_s, sin_s = _rope_tables(past_len + jnp.arange(ss, dtype=jnp.int32))
    tri = (jnp.arange(t_attn)[:, None] > jnp.arange(t_attn)[None, :]).astype(BF16)
    tri_u = (jnp.arange(t_cache)[:, None] < jnp.arange(t_cache)[None, :]).astype(BF16)
    sel = (jnp.arange(ss)[:, None] == (jnp.arange(SB_HEADS * ss)[None, :] % ss)).astype(BF16)
    cache_k = cache_sb_k.reshape(depth, bs, past_len, SB_W)
    cache_v = cache_sb_v.reshape(depth, bs, past_len, SB_W)

    ts_p = _largest_tile(sp, 512)
    ts_post = _largest_tile(sp, 1024)
    yp, ys = x_prompt, x_sample
    st_p, st_s = None, None
    for l in range(depth):
        w = _layer_weights(l, g_mix, w_in, g_sb_q, g_sb_k, g_qa, w_qb, g_kva, w_kvb, g_mq_nope, g_mk_nope,
                           g_mq_pe, g_mk_pe, w_out, g_ffn, w_rg, b_rg, w_re, b_re, w_gate, w_up, w_down)
        mod_p, mod_s = mod[l, :bp], mod[l, bp:]

        st_p, (qsb, ksb, vsb, qn, qpe, kn, vm, kpe) = _pre(yp, mod_p, cos_p, sin_p, w, 1, ts_p, l, depth, st_p)
        o = _attn_prompt(qsb, qn, qpe, ksb, vsb, kn, kpe, vm, tri)
        yp = _post(yp, o, mod_p, w, 1, ts_post)

        st_s, (qsb, ksb, vsb, qn, qpe, kn, vm, kpe) = _pre(ys, mod_s, cos_s, sin_s, w, bs, ss, l, depth, st_s)
        ckn, cvm, ckpe = _expand(l, cache_mla_ckv, cache_mla_kpe, w, _largest_tile(past_len, 1024))
        o = _attn_sample(l, qsb, qn, qpe, ksb, vsb, kn, kpe, vm, cache_k, cache_v, ckn, ckpe, cvm, tri_u, sel)
        ys = _post(ys, o, mod_s, w, bs, ss)

    heads = lambda a: a.reshape(*a.shape[:3], SB_HEADS, SB_DIM)
    return (yp, ys, heads(st_p[0]), heads(st_p[1]), st_p[2], st_p[3],
            heads(st_s[0]), heads(st_s[1]), st_s[2], st_s[3])
```

```python
import functools
import math

import jax
import jax.numpy as jnp
from jax import lax
from jax.experimental import pallas as pl
from jax.experimental.pallas import tpu as pltpu

F32 = jnp.float32
BF16 = jnp.bfloat16

EPS = 1e-6
CHUNK = 64
SB_HEADS = 8
SB_DIM = 64
SB_W = SB_HEADS * SB_DIM
MLA_HEADS = 8
MLA_NOPE = 64
MLA_ROPE = 32
MLA_V = 64
Q_RANK = 384
KV_RANK = 256
ROPE_THETA = 10000.0
MLA_W = MLA_HEADS * MLA_V
N_GROUPS = 4
EXP_PER_GROUP = 8
N_EXPERTS = N_GROUPS * EXP_PER_GROUP
D_EXPERT = 256

LANES = 128
PAIR_W = 2 * SB_DIM
N_PAIRS = SB_HEADS // 2
KPE_TILE = LANES // MLA_ROPE
PE_W = MLA_HEADS * MLA_ROPE
C_QSB, C_KSB, C_VSB = 0, SB_W, 2 * SB_W
C_QLAT = 3 * SB_W
C_KVLAT = C_QLAT + Q_RANK
C_KPE = C_KVLAT + KV_RANK
IN_COLS_PAD = C_KPE + LANES
MLA_SCALE = 1.0 / math.sqrt(MLA_NOPE + MLA_ROPE)
NEG_BIG = -1e30
GROUP_ALIGN = 256
SB_DEAD = -104.0
VMEM_LIMIT = 56 * 1024 * 1024


def _nt_dot(a, b):
    return lax.dot_general(a, b, (((1,), (1,)), ((), ())), preferred_element_type=F32)


def _dot(a, b):
    return jnp.dot(a, b, preferred_element_type=F32)


def _rms(x, g):
    ms = jnp.mean(x * x, axis=-1, keepdims=True)
    return x * lax.rsqrt(ms + EPS) * g


def _seg_rms(x, seg_mean, g):
    ms = _dot((x * x).astype(BF16), seg_mean)
    return x * lax.rsqrt(ms + EPS) * g


def _rot_half(x, lane_in_group_lt_half):
    w = x.shape[-1]
    fwd = pltpu.roll(x, w - MLA_ROPE // 2, axis=1)
    bwd = pltpu.roll(x, MLA_ROPE // 2, axis=1)
    return jnp.where(lane_in_group_lt_half, fwd, bwd)


def _ada_kernel(c_ref, w_ref, b_ref, o_ref):
    c = c_ref[...]
    s = c / (1.0 + jnp.exp(-c))
    o_ref[0] = _dot(s.astype(BF16), w_ref[0].astype(BF16)) + b_ref[0]


def _ada(c_all, w_ada, b_ada):
    depth, d, n = w_ada.shape
    nb = c_all.shape[0]
    tn = d
    return pl.pallas_call(
        _ada_kernel,
        out_shape=jax.ShapeDtypeStruct((depth, nb, n), F32),
        grid=(depth, n // tn),
        in_specs=[
            pl.BlockSpec((nb, d), lambda l, j: (0, 0)),
            pl.BlockSpec((1, d, tn), lambda l, j: (l, 0, j)),
            pl.BlockSpec((1, 1, tn), lambda l, j: (l, 0, j)),
        ],
        out_specs=pl.BlockSpec((1, nb, tn), lambda l, j: (l, 0, j)),
        name="ada",
        compiler_params=pltpu.CompilerParams(dimension_semantics=("arbitrary", "arbitrary")),
    )(c_all, w_ada, b_ada.reshape(depth, 1, n))


def _pre_kernel(x_ref, mod_ref, cos_ref, sin_ref, gmix_ref, win_ref, gsq_ref, gsk_ref, gqa_ref,
                wqn_ref, wqp_ref, gkva_ref, wkk_ref, wkv_ref, gqn_ref, gkn_ref, gqpe_ref, gkpe_ref,
                m64_ref, m32_ref,
                ksb_o, vsb_o, ckv_o, kpe_o,
                qsb_b, ksb_b, vsb_b, qn_b, qpe_b, kn_b, vm_b, kpe_b):
    nb, ts, d = x_ref.shape
    r = nb * ts
    x = x_ref[...]
    sh = mod_ref[:, 0:1, :]
    sc = mod_ref[:, 1:2, :]
    h = _rms(x, gmix_ref[...]) * (1.0 + sc) + sh
    u = _dot(h.reshape(r, d).astype(BF16), win_ref[...])

    m64 = m64_ref[...]
    q_sb = _seg_rms(u[:, C_QSB:C_QSB + SB_W], m64, gsq_ref[...])
    k_sb = _seg_rms(u[:, C_KSB:C_KSB + SB_W], m64, gsk_ref[...])
    v_sb = u[:, C_VSB:C_VSB + SB_W]
    ksb_o[0] = k_sb.reshape(nb, ts, SB_W)
    vsb_o[0] = v_sb.reshape(nb, ts, SB_W)
    qsb_b[...] = (q_sb * (1.0 / math.sqrt(SB_DIM))).astype(BF16).reshape(nb, ts, SB_W)
    ksb_b[...] = k_sb.astype(BF16).reshape(nb, ts, SB_W)
    vsb_b[...] = v_sb.astype(BF16).reshape(nb, ts, SB_W)

    cos = cos_ref[...]
    sin = sin_ref[...]
    lane = lax.broadcasted_iota(jnp.int32, (1, PE_W), 1)
    first_half = (lane & (MLA_ROPE - 1)) < (MLA_ROPE // 2)

    q_lat = _rms(u[:, C_QLAT:C_QLAT + Q_RANK], gqa_ref[...]).astype(BF16)
    qn = _seg_rms(_dot(q_lat, wqn_ref[...]), m64, gqn_ref[...])
    qn_b[...] = (qn * MLA_SCALE).astype(BF16).reshape(nb, ts, MLA_W)
    qp = _seg_rms(_dot(q_lat, wqp_ref[...]), m32_ref[...], gqpe_ref[...])
    qp_rot = _rot_half(qp, first_half)
    qp = qp.reshape(nb, ts, PE_W) * cos + qp_rot.reshape(nb, ts, PE_W) * sin
    qpe_b[...] = (qp * MLA_SCALE).astype(BF16)

    ckv = _rms(u[:, C_KVLAT:C_KVLAT + KV_RANK], gkva_ref[...])
    ckv_o[0] = ckv.reshape(nb, ts, KV_RANK)
    ckv16 = ckv.astype(BF16)
    kn = _seg_rms(_dot(ckv16, wkk_ref[...]), m64, gkn_ref[...])
    kn_b[...] = kn.astype(BF16).reshape(nb, ts, MLA_W)
    vm_b[...] = _dot(ckv16, wkv_ref[...]).astype(BF16).reshape(nb, ts, MLA_W)

    kp = _rms(u[:, C_KPE:C_KPE + LANES], gkpe_ref[...])
    kp_rot = _rot_half(kp, first_half[:, :LANES])
    kp = kp.reshape(nb, ts, LANES) * cos[:, :, :LANES] + kp_rot.reshape(nb, ts, LANES) * sin[:, :, :LANES]
    kpe_o[0] = kp[:, :, :MLA_ROPE]
    kpe_b[...] = kp.astype(BF16)


def _pre_kernel_aliased(*refs):
    _pre_kernel(*refs[N_STATE:])


N_STATE = 4


def _pre(x, mod, cos, sin, w, nb, ts, layer, depth, state):
    b, s, d = x.shape
    grid = (b // nb, s // ts)
    tok = lambda width: pl.BlockSpec((nb, ts, width), lambda i, j: (i, j, 0))
    slab = lambda width: pl.BlockSpec((1, nb, ts, width), lambda i, j: (layer, i, j, 0))
    const = lambda a: pl.BlockSpec(a.shape, lambda i, j: (0,) * a.ndim)
    weights = [w["g_mix"], w["w_in"], w["g_sb_q"], w["g_sb_k"], w["g_qa"], w["w_qn"], w["w_qp"], w["g_kva"],
               w["w_kk"], w["w_kv"], w["g_mq_nope"], w["g_mk_nope"], w["g_mq_pe"], w["g_mk_pe"],
               w["m64"], w["m32"]]
    state_widths = (SB_W, SB_W, KV_RANK, MLA_ROPE)
    bf16_widths = (SB_W, SB_W, SB_W, MLA_W, PE_W, MLA_W, MLA_W, LANES)
    out_shape = tuple(jax.ShapeDtypeStruct((depth, b, s, wd), F32) for wd in state_widths) + \
        tuple(jax.ShapeDtypeStruct((b, s, wd), BF16) for wd in bf16_widths)
    in_specs = [tok(d),
                pl.BlockSpec((nb, 6, d), lambda i, j: (i, 0, 0)),
                pl.BlockSpec((1, ts, PE_W), lambda i, j: (0, j, 0)),
                pl.BlockSpec((1, ts, PE_W), lambda i, j: (0, j, 0))] + [const(a) for a in weights]
    args = (x, mod, cos, sin, *weights)
    body, aliases = _pre_kernel, {}
    if state is not None:
        body, aliases = _pre_kernel_aliased, {i: i for i in range(N_STATE)}
        in_specs = [pl.BlockSpec(memory_space=pl.ANY)] * N_STATE + in_specs
        args = (*state, *args)
    outs = pl.pallas_call(
        body,
        out_shape=out_shape,
        grid=grid,
        in_specs=in_specs,
        out_specs=tuple(slab(wd) for wd in state_widths) + tuple(tok(wd) for wd in bf16_widths),
        input_output_aliases=aliases,
        name="pre",
        compiler_params=pltpu.CompilerParams(dimension_semantics=("arbitrary", "arbitrary"),
                                             vmem_limit_bytes=VMEM_LIMIT),
    )(*args)
    return outs[:N_STATE], outs[N_STATE:]


def _expand_kernel(ckv_ref, kpe_ref, wkk_ref, wkv_ref, gkn_ref, m64_ref, rep_ref, kn_b, vm_b, kpe_b):
    ckv16 = ckv_ref[0, 0].astype(BF16)
    kn = _seg_rms(_dot(ckv16, wkk_ref[...]), m64_ref[...], gkn_ref[...])
    kn_b[0] = kn.astype(BF16)
    vm_b[0] = _dot(ckv16, wkv_ref[...]).astype(BF16)
    kpe_b[0] = _dot(kpe_ref[0, 0].astype(BF16), rep_ref[...]).astype(BF16)


def _expand(layer, ckv, kpe, w, tp):
    _, b, p, _ = ckv.shape
    const = lambda a: pl.BlockSpec(a.shape, lambda i, j: (0,) * a.ndim)
    weights = [w["w_kk"], w["w_kv"], w["g_mk_nope"], w["m64"], w["rep"]]
    return pl.pallas_call(
        _expand_kernel,
        out_shape=(jax.ShapeDtypeStruct((b, p, MLA_W), BF16), jax.ShapeDtypeStruct((b, p, MLA_W), BF16),
                   jax.ShapeDtypeStruct((b, p, LANES), BF16)),
        grid=(b, p // tp),
        in_specs=[pl.BlockSpec((1, 1, tp, KV_RANK), lambda i, j: (layer, i, j, 0)),
                  pl.BlockSpec((1, 1, tp, MLA_ROPE), lambda i, j: (layer, i, j, 0))] + [const(a) for a in weights],
        out_specs=(pl.BlockSpec((1, tp, MLA_W), lambda i, j: (i, j, 0)),
                   pl.BlockSpec((1, tp, MLA_W), lambda i, j: (i, j, 0)),
                   pl.BlockSpec((1, tp, LANES), lambda i, j: (i, j, 0))),
        name="expand",
        compiler_params=pltpu.CompilerParams(dimension_semantics=("arbitrary", "arbitrary"),
                                             vmem_limit_bytes=VMEM_LIMIT),
    )(ckv, kpe, *weights)


def _stick_logs(z, mask):
    sign_bit = jnp.uint32(0x80000000)
    neg_abs = lax.bitcast_convert_type(lax.bitcast_convert_type(z, jnp.uint32) | sign_bit, F32)
    log_beta = jnp.minimum(z, 0.0) - jnp.log(1.0 + jnp.exp(neg_abs))
    log_1m = log_beta - z
    if mask is not None:
        log_1m = jnp.where(mask, log_1m, 0.0)
    return log_beta, log_1m


def _head_operands(qsb_ref, qn_ref, qpe_ref, h):
    p, e = divmod(h, 2)
    lane = lax.broadcasted_iota(jnp.int32, (1, LANES), 1)
    half = (lane >= SB_DIM) if e == 1 else (lane < SB_DIM)
    sl = slice(PAIR_W * p, PAIR_W * (p + 1))
    q_sb = jnp.where(half, qsb_ref[:, sl], jnp.zeros((), BF16))
    q_n = jnp.where(half, qn_ref[:, sl], jnp.zeros((), BF16))
    slab, pos = divmod(h, KPE_TILE)
    pe_lanes = (lane >= MLA_ROPE * pos) & (lane < MLA_ROPE * (pos + 1))
    q_pe = jnp.where(pe_lanes, qpe_ref[:, LANES * slab:LANES * (slab + 1)], jnp.zeros((), BF16))
    return q_sb, jnp.concatenate([q_n, q_pe], axis=-1)


def _attend_block(q_ops, k_sb, v_sb, k_n, k_pe, v_m, tri, sb_mask, mla_mask, first,
                  acc_sb, acc_m, c_s, m_s, with_sb=True):
    heads = range(SB_HEADS)
    slab = lambda a, h: a[:, PAIR_W * (h // 2):PAIR_W * (h // 2 + 1)]
    kcat = [jnp.concatenate([k_n[:, PAIR_W * p:PAIR_W * (p + 1)], k_pe], axis=-1) for p in range(N_PAIRS)]

    z_sb = [_nt_dot(q_ops[h][0], slab(k_sb, h)) for h in heads] if with_sb else []
    z_m = [_nt_dot(q_ops[h][1], kcat[h // 2]) for h in heads]

    log_w, suffix = [], []
    for h in heads if with_sb else ():
        log_beta, log_1m = _stick_logs(z_sb[h], sb_mask)
        suffix.append(_dot(log_1m.astype(BF16), tri))
        row_sum = jnp.sum(log_1m, axis=-1, keepdims=True)
        if first:
            log_w.append(log_beta)
            c_s[h] = jnp.broadcast_to(row_sum, (log_1m.shape[0], LANES))
        else:
            carry = c_s[h]
            log_w.append(log_beta + jnp.concatenate([carry] * (log_beta.shape[1] // LANES), axis=-1))
            c_s[h] = carry + row_sum

    lane = lax.broadcasted_iota(jnp.int32, (1, LANES), 1)
    for h in heads:
        z = z_m[h]
        if mla_mask is not None:
            z = jnp.where(mla_mask, z, NEG_BIG)
        tq = z.shape[0]
        halves = [z[:, LANES * i:LANES * (i + 1)] for i in range(z.shape[1] // LANES)]
        zmax = jnp.max(functools.reduce(jnp.maximum, halves), axis=-1, keepdims=True)
        own_half = (lane >= SB_DIM) if h % 2 else (lane < SB_DIM)
        v_ones = jnp.where(own_half, slab(v_m, h), jnp.ones((), BF16))
        if first:
            m_new = jnp.broadcast_to(zmax, (tq, LANES))
        else:
            m_old = m_s[h]
            m_new = jnp.maximum(m_old, zmax)
        prob = jnp.concatenate([jnp.exp(zh - m_new) for zh in halves], axis=-1).astype(BF16)
        m_s[h] = m_new
        if first:
            acc_m[h] = _dot(prob, v_ones)
        else:
            acc_m[h] = jnp.exp(m_old - m_new) * acc_m[h] + _dot(prob, v_ones)

    for h in heads if with_sb else ():
        wgt = jnp.exp(log_w[h] + suffix[h])
        if sb_mask is not None:
            wgt = jnp.where(sb_mask, wgt, 0.0)
        pv = _dot(wgt.astype(BF16), slab(v_sb, h))
        if first:
            acc_sb[h] = pv
        else:
            acc_sb[h] += pv


def _write_out(o_ref, acc_sb, acc_m):
    lane = lax.broadcasted_iota(jnp.int32, (1, LANES), 1)
    lo_half = lane < SB_DIM
    for p in range(N_PAIRS):
        h0, h1 = 2 * p, 2 * p + 1
        o_ref[:, PAIR_W * p:PAIR_W * (p + 1)] = jnp.where(lo_half, acc_sb[h0], acc_sb[h1]).astype(o_ref.dtype)
        a0, a1 = acc_m[h0], acc_m[h1]
        mla = jnp.where(lo_half, a0 / pltpu.roll(a0, SB_DIM, axis=1), a1 / pltpu.roll(a1, SB_DIM, axis=1))
        o_ref[:, SB_W + PAIR_W * p:SB_W + PAIR_W * (p + 1)] = mla.astype(o_ref.dtype)


def _diag_masks(tq, tk, q_pos0, k_pos0):
    qpos = q_pos0 + lax.broadcasted_iota(jnp.int32, (tq, tk), 0)
    kpos = k_pos0 + lax.broadcasted_iota(jnp.int32, (tq, tk), 1)
    shift = CHUNK.bit_length() - 1
    return kpos < qpos, (kpos >> shift) <= (qpos >> shift)


def _attn_prompt_kernel(qsb_ref, qn_ref, qpe_ref, ksb_ref, vsb_ref, kn_ref, kpe_ref, vm_ref, tri_ref,
                        o_ref, acc_sb, acc_m, c_s, m_s):
    t = qsb_ref.shape[1]
    qi = pl.program_id(1)
    q_ops = [_head_operands(qsb_ref.at[0], qn_ref.at[0], qpe_ref.at[0], h) for h in range(SB_HEADS)]
    tri = tri_ref[...]

    def key_block(start):
        rows = pl.ds(start, t)
        return (ksb_ref[0, rows, :], vsb_ref[0, rows, :], kn_ref[0, rows, :], kpe_ref[0, rows, :],
                vm_ref[0, rows, :])

    sb_mask, mla_mask = _diag_masks(t, t, 0, 0)
    k_sb, v_sb, k_n, k_pe, v_m = key_block(pl.multiple_of(qi * t, t))
    _attend_block(q_ops, k_sb, v_sb, k_n, k_pe, v_m, tri, sb_mask, mla_mask, True,
                  acc_sb, acc_m, c_s, m_s)

    def earlier_block(j, with_sb):
        kb = qi - 1 - j
        k_sb, v_sb, k_n, k_pe, v_m = key_block(pl.multiple_of(kb * t, t))
        _attend_block(q_ops, k_sb, v_sb, k_n, k_pe, v_m, tri, None, None, False,
                      acc_sb, acc_m, c_s, m_s, with_sb=with_sb)

    def sb_alive():
        return (jnp.max(c_s[...]) > SB_DEAD).astype(jnp.int32)

    def sb_cond(state):
        j, alive = state
        return jnp.logical_and(j < qi, alive > 0)

    def sb_body(state):
        j, _ = state
        earlier_block(j, True)
        return j + 1, sb_alive()

    j_done, _ = lax.while_loop(sb_cond, sb_body, (jnp.int32(0), sb_alive()))

    def mla_body(j, _):
        earlier_block(j, False)
        return 0

    lax.fori_loop(j_done, qi, mla_body, 0)
    _write_out(o_ref.at[0], acc_sb, acc_m)


def _attn_prompt(qsb, qn, qpe, ksb, vsb, kn, kpe, vm, tri):
    b, s, _ = qsb.shape
    t = tri.shape[0]
    qblk = lambda width: pl.BlockSpec((1, t, width), lambda i, j: (i, j, 0))
    full = lambda width: pl.BlockSpec((1, s, width), lambda i, j: (i, 0, 0))
    return pl.pallas_call(
        _attn_prompt_kernel,
        out_shape=jax.ShapeDtypeStruct((b, s, SB_W + MLA_W), BF16),
        grid=(b, s // t),
        in_specs=[qblk(SB_W), qblk(MLA_W), qblk(PE_W), full(SB_W), full(SB_W), full(MLA_W), full(LANES),
                  full(MLA_W), pl.BlockSpec((t, t), lambda i, j: (0, 0))],
        out_specs=qblk(SB_W + MLA_W),
        scratch_shapes=[pltpu.VMEM((SB_HEADS, t, LANES), F32)] * 4,
        name="attn_prompt",
        compiler_params=pltpu.CompilerParams(dimension_semantics=("arbitrary", "arbitrary"),
                                             vmem_limit_bytes=VMEM_LIMIT),
    )(qsb, qn, qpe, ksb, vsb, kn, kpe, vm, tri)


def _tn_dot(a, b):
    return lax.dot_general(a, b, (((0,), (0,)), ((), ())), preferred_element_type=F32)


def _sample_block(k_sb, v_sb, k_cat, v_m, tri_u, sb_mask, mla_mask, first,
                  qbd_sb, qbd_m, acc_sb, acc_m, c_s, m_s, l_s):
    z = _dot(k_sb, qbd_sb[...])
    log_beta, log_1m = _stick_logs(z, sb_mask)
    suffix = _dot(tri_u, log_1m.astype(BF16))
    col_sum = jnp.sum(log_1m, axis=0, keepdims=True)
    if first:
        wgt = jnp.exp(log_beta + suffix)
        c_s[...] = col_sum
    else:
        carry = c_s[...]
        wgt = jnp.exp(log_beta + suffix + carry)
        c_s[...] = carry + col_sum
    if sb_mask is not None:
        wgt = jnp.where(sb_mask, wgt, 0.0)
    pv = _tn_dot(v_sb, wgt.astype(BF16))
    if first:
        acc_sb[...] = pv
    else:
        acc_sb[...] += pv

    z = _dot(k_cat, qbd_m[...])
    if mla_mask is not None:
        z = jnp.where(mla_mask, z, NEG_BIG)
    zmax = jnp.max(z, axis=0, keepdims=True)
    if first:
        prob = jnp.exp(z - zmax)
        m_s[...] = zmax
        l_s[...] = jnp.sum(prob, axis=0, keepdims=True)
        acc_m[...] = _tn_dot(v_m, prob.astype(BF16))
    else:
        m_old = m_s[...]
        m_new = jnp.maximum(m_old, zmax)
        prob = jnp.exp(z - m_new)
        alpha = jnp.exp(m_old - m_new)
        m_s[...] = m_new
        l_s[...] = alpha * l_s[...] + jnp.sum(prob, axis=0, keepdims=True)
        acc_m[...] = alpha * acc_m[...] + _tn_dot(v_m, prob.astype(BF16))


def _attn_sample_kernel(qsb_ref, qn_ref, qpe_ref, ksb_ref, vsb_ref, kn_ref, kpe_ref, vm_ref,
                        cksb_ref, cvsb_ref, ckn_ref, ckpe_ref, cvm_ref, tri_ref, sel_ref,
                        o_ref, qbd_sb, qbd_m, acc_sb, acc_m, c_s, m_s, l_s, *, past_len):
    tq = qsb_ref.shape[1]
    ncol = SB_HEADS * tq
    q_shift = tq.bit_length() - 1
    j = pl.program_id(1)
    state = (qbd_sb, qbd_m, acc_sb, acc_m, c_s, m_s, l_s)

    @pl.when(j == 0)
    def _():
        sel = sel_ref[...]
        row_head = lax.broadcasted_iota(jnp.int32, (SB_W, ncol), 0) >> (SB_DIM.bit_length() - 1)
        col_head = lax.broadcasted_iota(jnp.int32, (SB_W, ncol), 1) >> q_shift
        on_diag = row_head == col_head
        qbd_sb[...] = jnp.where(on_diag, _tn_dot(qsb_ref[0], sel), 0.0).astype(BF16)
        qbd_m[0:MLA_W, :] = jnp.where(on_diag, _tn_dot(qn_ref[0], sel), 0.0).astype(BF16)
        pe_t = _tn_dot(qpe_ref[0], sel)
        head_of_col = lax.broadcasted_iota(jnp.int32, (MLA_ROPE, ncol), 1) >> q_shift
        pe = jnp.zeros((MLA_ROPE, ncol), F32)
        for h in range(MLA_HEADS):
            pe = pe + jnp.where(head_of_col == h, pe_t[MLA_ROPE * h:MLA_ROPE * (h + 1), :], 0.0)
        qbd_m[MLA_W:MLA_W + LANES, :] = jnp.concatenate(
            [pe, jnp.zeros((LANES - MLA_ROPE, ncol), F32)], axis=0).astype(BF16)

        key_i = lax.broadcasted_iota(jnp.int32, (tq, ncol), 0)
        qry_i = lax.broadcasted_iota(jnp.int32, (tq, ncol), 1) & (tq - 1)
        c_shift = CHUNK.bit_length() - 1
        sb_mask = key_i < qry_i
        mla_mask = ((past_len + key_i) >> c_shift) <= ((past_len + qry_i) >> c_shift)
        k_cat = jnp.concatenate([kn_ref[0], kpe_ref[0]], axis=-1)
        _sample_block(ksb_ref[0], vsb_ref[0], k_cat, vm_ref[0], tri_ref[0:tq, 0:tq], sb_mask, mla_mask, True,
                      *state)

    k_cat = jnp.concatenate([ckn_ref[0], ckpe_ref[0]], axis=-1)
    _sample_block(cksb_ref[0, 0].astype(BF16), cvsb_ref[0, 0].astype(BF16), k_cat, cvm_ref[0], tri_ref[...],
                  None, None, False, *state)

    @pl.when(j == pl.num_programs(1) - 1)
    def _():
        lane_head = lax.broadcasted_iota(jnp.int32, (tq, SB_W), 1) >> (SB_DIM.bit_length() - 1)

        def diag_blocks(acc_t):
            acc = acc_t.T
            out = jnp.zeros((tq, SB_W), F32)
            for h in range(SB_HEADS):
                out = out + jnp.where(lane_head == h, acc[tq * h:tq * (h + 1), :], 0.0)
            return out

        o_ref[0, :, 0:SB_W] = diag_blocks(acc_sb[...]).astype(o_ref.dtype)
        o_ref[0, :, SB_W:SB_W + MLA_W] = diag_blocks(acc_m[...] / l_s[...]).astype(o_ref.dtype)


def _attn_sample(layer, qsb, qn, qpe, ksb, vsb, kn, kpe, vm, cache_k, cache_v, ckn, ckpe, cvm, tri_u, sel):
    b, tq, _ = qsb.shape
    past_len = cache_k.shape[2]
    tk = tri_u.shape[0]
    nkb = past_len // tk
    ncol = SB_HEADS * tq
    new = lambda width: pl.BlockSpec((1, tq, width), lambda i, j: (i, 0, 0))
    rev = lambda width: pl.BlockSpec((1, tk, width), lambda i, j: (i, nkb - 1 - j, 0))
    rev_l = lambda width: pl.BlockSpec((1, 1, tk, width), lambda i, j: (layer, i, nkb - 1 - j, 0))
    const = lambda a: pl.BlockSpec(a.shape, lambda i, j: (0,) * a.ndim)
    return pl.pallas_call(
        functools.partial(_attn_sample_kernel, past_len=past_len),
        out_shape=jax.ShapeDtypeStruct((b, tq, SB_W + MLA_W), BF16),
        grid=(b, nkb),
        in_specs=[new(SB_W), new(MLA_W), new(PE_W), new(SB_W), new(SB_W), new(MLA_W), new(LANES), new(MLA_W),
                  rev_l(SB_W), rev_l(SB_W), rev(MLA_W), rev(LANES), rev(MLA_W), const(tri_u), const(sel)],
        out_specs=new(SB_W + MLA_W),
        scratch_shapes=[pltpu.VMEM((SB_W, ncol), BF16), pltpu.VMEM((MLA_W + LANES, ncol), BF16),
                        pltpu.VMEM((SB_W, ncol), F32), pltpu.VMEM((MLA_W, ncol), F32),
                        pltpu.VMEM((1, ncol), F32), pltpu.VMEM((1, ncol), F32), pltpu.VMEM((1, ncol), F32)],
        name="attn_sample",
        compiler_params=pltpu.CompilerParams(dimension_semantics=("arbitrary", "arbitrary"),
                                             vmem_limit_bytes=VMEM_LIMIT),
    )(qsb, qn, qpe, ksb, vsb, kn, kpe, vm, cache_k, cache_v, ckn, ckpe, cvm, tri_u, sel)


def _post_kernel(x_ref, o_ref, mod_ref, wout_ref, gffn_ref, wr_ref, br_ref, ltri_ref, wg_ref, wu_ref, wd_ref,
                 out_ref, hs_s, gs_s, perm_s, y_s, meta_s):
    nb, ts, d = x_ref.shape
    r = nb * ts
    e = pl.program_id(2)
    group_shift = EXP_PER_GROUP.bit_length() - 1

    @pl.when(e == 0)
    def _():
        gt_a = mod_ref[:, 2:3, :]
        sh_m = mod_ref[:, 3:4, :]
        sc_m = mod_ref[:, 4:5, :]
        att = _dot(o_ref[...].reshape(r, o_ref.shape[-1]), wout_ref[...])
        x1 = x_ref[...] + gt_a * att.reshape(nb, ts, d)
        out_ref[...] = x1
        h2 = (_rms(x1, gffn_ref[...]) * (1.0 + sc_m) + sh_m).reshape(r, d).astype(BF16)
        logits = _dot(h2, wr_ref[...]) + br_ref[...]
        lane_i = lax.broadcasted_iota(jnp.int32, (r, LANES), 1)
        lane = lane_i.astype(F32)
        group_of_lane = (lane_i >> group_shift).astype(F32)
        neg_inf = jnp.float32(-jnp.inf)
        no_lane = jnp.float32(LANES)
        gl = jnp.where(lane_i < N_GROUPS, logits[:, LANES:], neg_inf)
        gmax = jnp.max(gl, axis=-1, keepdims=True)
        gsel = jnp.min(jnp.where(gl == gmax, lane, no_lane), axis=-1, keepdims=True)
        p_g = 1.0 / jnp.sum(jnp.exp(gl - gmax), axis=-1, keepdims=True)
        in_group = group_of_lane == gsel
        em = jnp.where(in_group, logits[:, :LANES], neg_inf)
        v1 = jnp.max(em, axis=-1, keepdims=True)
        i1 = jnp.min(jnp.where(em == v1, lane, no_lane), axis=-1, keepdims=True)
        em2 = jnp.where(lane == i1, neg_inf, em)
        v2 = jnp.max(em2, axis=-1, keepdims=True)
        i2 = jnp.min(jnp.where(em2 == v2, lane, no_lane), axis=-1, keepdims=True)
        t = jnp.exp(v2 - v1)
        w1 = p_g / (1.0 + t)
        first_of_group = gsel * float(EXP_PER_GROUP)
        gates = jnp.where(lane == i1 - first_of_group, w1, 0.0) + jnp.where(lane == i2 - first_of_group, w1 * t, 0.0)

        in_g = jnp.where(lane == gsel, 1.0, 0.0)
        count = jnp.sum(in_g, axis=0, keepdims=True)
        earlier = _dot(ltri_ref[...], in_g.astype(BF16))
        lane_row = lax.broadcasted_iota(jnp.int32, (1, LANES), 1)
        start = jnp.int32(0)
        start_of = jnp.zeros((1, LANES), F32)
        for g in range(N_GROUPS):
            n_g = jnp.sum(jnp.where(lane_row == g, count, 0.0)).astype(jnp.int32)
            n_tiles = (n_g + (GROUP_ALIGN - 1)) // GROUP_ALIGN
            meta_s[g] = start
            meta_s[N_GROUPS + g] = n_tiles
            start_of = jnp.where(lane_row == g, start.astype(F32), start_of)
            start = start + n_tiles * GROUP_ALIGN
        pos = jnp.sum(in_g * (start_of + earlier), axis=-1, keepdims=True)
        pos_row = jnp.broadcast_to(pos, (r, LANES)).T[0:1, :]
        n_rows = perm_s.shape[0]
        chunk = 256
        for c in range(n_rows // chunk):
            row = (c * chunk + lax.broadcasted_iota(jnp.int32, (chunk, r), 0)).astype(F32)
            perm_s[c * chunk:(c + 1) * chunk, :] = jnp.where(row == pos_row, 1.0, 0.0).astype(BF16)
        perm = perm_s[...]
        hs_s[...] = _dot(perm, h2).astype(BF16)
        g_hi = gates.astype(BF16)
        g_lo = (gates - g_hi.astype(F32)).astype(BF16)
        gs_s[...] = _dot(perm, g_hi) + _dot(perm, g_lo)
        y_s[...] = jnp.zeros(y_s.shape, F32)

    g = e >> group_shift
    j = e & (EXP_PER_GROUP - 1)
    bucket_start = meta_s[g]
    lane = lax.broadcasted_iota(jnp.int32, (GROUP_ALIGN, LANES), 1)

    def row_tile(i, _):
        rows = pl.ds(pl.multiple_of(bucket_start + i * GROUP_ALIGN, GROUP_ALIGN), GROUP_ALIGN)
        xs = hs_s[rows, :]
        a = _dot(xs, wg_ref[0])
        bb = _dot(xs, wu_ref[0])
        g_e = jnp.sum(jnp.where(lane == j, gs_s[rows, :], 0.0), axis=-1, keepdims=True)
        act = (a / (1.0 + jnp.exp(-a))) * bb * g_e
        y_s[rows, :] += _dot(act.astype(BF16), wd_ref[0])
        return 0

    lax.fori_loop(0, meta_s[N_GROUPS + g], row_tile, 0)

    @pl.when(e == pl.num_programs(2) - 1)
    def _():
        gt_m = mod_ref[:, 5:6, :]
        y = _tn_dot(perm_s[...], y_s[...].astype(BF16))
        out_ref[...] += gt_m * y.reshape(nb, ts, d)


def _post(x, o, mod, w, nb, ts):
    b, s, d = x.shape
    r = nb * ts
    n_rows = r + N_GROUPS * GROUP_ALIGN
    n_rows = -(-n_rows // 256) * 256
    ltri = (jnp.arange(r)[:, None] > jnp.arange(r)[None, :]).astype(BF16)
    tok = lambda width: pl.BlockSpec((nb, ts, width), lambda i, j, e: (i, j, 0))
    const = lambda a: pl.BlockSpec(a.shape, lambda i, j, e: (0,) * a.ndim, pipeline_mode=pl.Buffered(1))
    per_e = lambda a: pl.BlockSpec((1,) + a.shape[1:], lambda i, j, e: (e, 0, 0))
    return pl.pallas_call(
        _post_kernel,
        out_shape=jax.ShapeDtypeStruct((b, s, d), F32),
        grid=(b // nb, s // ts, N_EXPERTS),
        in_specs=[pl.BlockSpec((nb, ts, d), lambda i, j, e: (i, j, 0), pipeline_mode=pl.Buffered(1)),
                  tok(o.shape[-1]), pl.BlockSpec((nb, 6, d), lambda i, j, e: (i, 0, 0)),
                  const(w["w_out"]), const(w["g_ffn"]), const(w["w_r"]), const(w["b_r"]), const(ltri),
                  per_e(w["w_gate"]), per_e(w["w_up"]), per_e(w["w_down"])],
        out_specs=tok(d),
        scratch_shapes=[pltpu.VMEM((n_rows, d), BF16), pltpu.VMEM((n_rows, LANES), F32),
                        pltpu.VMEM((n_rows, r), BF16), pltpu.VMEM((n_rows, d), F32),
                        pltpu.SMEM((2 * N_GROUPS,), jnp.int32)],
        name="post",
        compiler_params=pltpu.CompilerParams(dimension_semantics=("arbitrary", "arbitrary", "arbitrary"),
                                             vmem_limit_bytes=VMEM_LIMIT),
    )(x, o, mod, w["w_out"], w["g_ffn"], w["w_r"], w["b_r"], ltri, w["w_gate"], w["w_up"], w["w_down"])


def _block_mean(width, seg):
    idx = jnp.arange(width) // seg
    return jnp.where(idx[:, None] == idx[None, :], 1.0 / seg, 0.0).astype(BF16)


def _layer_weights(l, g_mix, w_in, g_sb_q, g_sb_k, g_qa, w_qb, g_kva, w_kvb, g_mq_nope, g_mk_nope, g_mq_pe,
                   g_mk_pe, w_out, g_ffn, w_rg, b_rg, w_re, b_re, w_gate, w_up, w_down):
    d = w_in.shape[1]
    row = lambda g, reps: jnp.tile(g[l], reps).reshape(1, -1)
    kpe_cols = jnp.tile(w_in[l][:, C_KPE:C_KPE + MLA_ROPE], (1, KPE_TILE))
    w_qb3 = w_qb[l].reshape(Q_RANK, MLA_HEADS, MLA_NOPE + MLA_ROPE)
    w_kvb3 = w_kvb[l].reshape(KV_RANK, MLA_HEADS, MLA_NOPE + MLA_V)
    pad = jnp.zeros((d, LANES), F32)
    w_r = jnp.concatenate([pad.at[:, :N_EXPERTS].set(w_re[l]), pad.at[:, :N_GROUPS].set(w_rg[l])], axis=1)
    zrow = jnp.zeros((LANES,), F32)
    b_r = jnp.concatenate([zrow.at[:N_EXPERTS].set(b_re[l]), zrow.at[:N_GROUPS].set(b_rg[l])]).reshape(1, -1)
    rep = (jnp.arange(MLA_ROPE)[:, None] == (jnp.arange(LANES)[None, :] % MLA_ROPE)).astype(BF16)
    return dict(
        g_mix=row(g_mix, 1),
        w_in=jnp.concatenate([w_in[l][:, :C_KPE], kpe_cols], axis=1).astype(BF16),
        g_sb_q=row(g_sb_q, SB_HEADS), g_sb_k=row(g_sb_k, SB_HEADS), g_qa=row(g_qa, 1),
        w_qn=w_qb3[:, :, :MLA_NOPE].reshape(Q_RANK, MLA_W).astype(BF16),
        w_qp=w_qb3[:, :, MLA_NOPE:].reshape(Q_RANK, PE_W).astype(BF16),
        g_kva=row(g_kva, 1),
        w_kk=w_kvb3[:, :, :MLA_NOPE].reshape(KV_RANK, MLA_W).astype(BF16),
        w_kv=w_kvb3[:, :, MLA_NOPE:].reshape(KV_RANK, MLA_W).astype(BF16),
        g_mq_nope=row(g_mq_nope, MLA_HEADS), g_mk_nope=row(g_mk_nope, MLA_HEADS),
        g_mq_pe=row(g_mq_pe, MLA_HEADS), g_mk_pe=row(g_mk_pe, KPE_TILE),
        m64=_block_mean(SB_W, SB_DIM), m32=_block_mean(PE_W, MLA_ROPE), rep=rep,
        w_out=w_out[l].astype(BF16), g_ffn=row(g_ffn, 1), w_r=w_r.astype(BF16), b_r=b_r,
        w_gate=w_gate[l].astype(BF16), w_up=w_up[l].astype(BF16), w_down=w_down[l].astype(BF16),
    )


def _rope_tables(pos):
    inv = ROPE_THETA ** (-jnp.arange(0, MLA_ROPE, 2, dtype=F32) / MLA_ROPE)
    ang = pos.astype(F32)[:, None] * inv[None, :]
    cos, sin = jnp.cos(ang), jnp.sin(ang)
    cos = jnp.tile(jnp.concatenate([cos, cos], axis=-1), (1, MLA_HEADS))
    sin = jnp.tile(jnp.concatenate([-sin, sin], axis=-1), (1, MLA_HEADS))
    return cos[None], sin[None]


def _largest_tile(n, cap):
    t = min(n, cap)
    while n % t:
        t -= 8
    return t


def kernel(x_prompt, x_sample, cache_sb_k, cache_sb_v, cache_mla_ckv, cache_mla_kpe, c_prompt, c_sample, w_ada, b_ada, g_mix, w_in, g_sb_q, g_sb_k, g_qa, w_qb, g_kva, w_kvb, g_mq_nope, g_mk_nope, g_mq_pe, g_mk_pe, w_out, g_ffn, w_rg, b_rg, w_re, b_re, w_gate, w_up, w_down):
    bp, sp, d = x_prompt.shape
    bs, ss, _ = x_sample.shape
    depth = w_ada.shape[0]
    past_len = cache_sb_k.shape[2]
    t_attn = 256
    t_cache = 512
    assert sp % t_attn == 0 and past_len % t_cache == 0 and ss % 8 == 0 and ss & (ss - 1) == 0

    mod = _ada(jnp.concatenate([c_prompt, c_sample], axis=0), w_ada, b_ada).reshape(depth, bp + bs, 6, d)
    cos_p, sin_p = _rope_tables(jnp.arange(sp, dtype=jnp.int32))
    cos_s, sin_s = _rope_tables(past_len + jnp.arange(ss, dtype=jnp.int32))
    tri = (jnp.arange(t_attn)[:, None] > jnp.arange(t_attn)[None, :]).astype(BF16)
    tri_u = (jnp.arange(t_cache)[:, None] < jnp.arange(t_cache)[None, :]).astype(BF16)
    sel = (jnp.arange(ss)[:, None] == (jnp.arange(SB_HEADS * ss)[None, :] % ss)).astype(BF16)
    cache_k = cache_sb_k.reshape(depth, bs, past_len, SB_W)
    cache_v = cache_sb_v.reshape(depth, bs, past_len, SB_W)

    ts_p = _largest_tile(sp, 512)
    ts_post = _largest_tile(sp, 1024)
    yp, ys = x_prompt, x_sample
    st_p, st_s = None, None
    for l in range(depth):
        w = _layer_weights(l, g_mix, w_in, g_sb_q, g_sb_k, g_qa, w_qb, g_kva, w_kvb, g_mq_nope, g_mk_nope,
                           g_mq_pe, g_mk_pe, w_out, g_ffn, w_rg, b_rg, w_re, b_re, w_gate, w_up, w_down)
        mod_p, mod_s = mod[l, :bp], mod[l, bp:]

        st_p, (qsb, ksb, vsb, qn, qpe, kn, vm, kpe) = _pre(yp, mod_p, cos_p, sin_p, w, 1, ts_p, l, depth, st_p)
        o = _attn_prompt(qsb, qn, qpe, ksb, vsb, kn, kpe, vm, tri)
        yp = _post(yp, o, mod_p, w, 1, ts_post)

        st_s, (qsb, ksb, vsb, qn, qpe, kn, vm, kpe) = _pre(ys, mod_s, cos_s, sin_s, w, bs, ss, l, depth, st_s)
        ckn, cvm, ckpe = _expand(l, cache_mla_ckv, cache_mla_kpe, w, _largest_tile(past_len, 1024))
        o = _attn_sample(l, qsb, qn, qpe, ksb, vsb, kn, kpe, vm, cache_k, cache_v, ckn, ckpe, cvm, tri_u, sel)
        ys = _post(ys, o, mod_s, w, bs, ss)

    heads = lambda a: a.reshape(*a.shape[:3], SB_HEADS, SB_DIM)
    return (yp, ys, heads(st_p[0]), heads(st_p[1]), st_p[2], st_p[3],
            heads(st_s[0]), heads(st_s[1]), st_s[2], st_s[3])
```

```python
import functools
import math

import jax
import jax.numpy as jnp
from jax import lax
from jax.experimental import pallas as pl
from jax.experimental.pallas import tpu as pltpu

F32 = jnp.float32
BF16 = jnp.bfloat16

EPS = 1e-6
CHUNK = 64
SB_HEADS = 8
SB_DIM = 64
SB_W = SB_HEADS * SB_DIM
MLA_HEADS = 8
MLA_NOPE = 64
MLA_ROPE = 32
MLA_V = 64
Q_RANK = 384
KV_RANK = 256
ROPE_THETA = 10000.0
MLA_W = MLA_HEADS * MLA_V
N_GROUPS = 4
EXP_PER_GROUP = 8
N_EXPERTS = N_GROUPS * EXP_PER_GROUP
D_EXPERT = 256

LANES = 128
PAIR_W = 2 * SB_DIM
N_PAIRS = SB_HEADS // 2
KPE_TILE = LANES // MLA_ROPE
PE_W = MLA_HEADS * MLA_ROPE
C_QSB, C_KSB, C_VSB = 0, SB_W, 2 * SB_W
C_QLAT = 3 * SB_W
C_KVLAT = C_QLAT + Q_RANK
C_KPE = C_KVLAT + KV_RANK
IN_COLS_PAD = C_KPE + LANES
MLA_SCALE = 1.0 / math.sqrt(MLA_NOPE + MLA_ROPE)
NEG_BIG = -1e30
EXPERTS_PER_STEP = 2
GROUP_ALIGN = 256
SB_DEAD = -104.0
VMEM_LIMIT = 56 * 1024 * 1024


def _nt_dot(a, b):
    return lax.dot_general(a, b, (((1,), (1,)), ((), ())), preferred_element_type=F32)


def _dot(a, b):
    return jnp.dot(a, b, preferred_element_type=F32)


def _rms(x, g):
    ms = jnp.mean(x * x, axis=-1, keepdims=True)
    return x * lax.rsqrt(ms + EPS) * g


def _seg_rms(x, seg_mean, g):
    ms = _dot((x * x).astype(BF16), seg_mean)
    return x * lax.rsqrt(ms + EPS) * g


def _rot_half(x, lane_in_group_lt_half):
    w = x.shape[-1]
    fwd = pltpu.roll(x, w - MLA_ROPE // 2, axis=1)
    bwd = pltpu.roll(x, MLA_ROPE // 2, axis=1)
    return jnp.where(lane_in_group_lt_half, fwd, bwd)


def _ada_kernel(c_ref, w_ref, b_ref, o_ref):
    c = c_ref[...]
    s = c / (1.0 + jnp.exp(-c))
    o_ref[0] = _dot(s.astype(BF16), w_ref[0].astype(BF16)) + b_ref[0]


def _ada(c_all, w_ada, b_ada):
    depth, d, n = w_ada.shape
    nb = c_all.shape[0]
    tn = d
    return pl.pallas_call(
        _ada_kernel,
        out_shape=jax.ShapeDtypeStruct((depth, nb, n), F32),
        grid=(depth, n // tn),
        in_specs=[
            pl.BlockSpec((nb, d), lambda l, j: (0, 0)),
            pl.BlockSpec((1, d, tn), lambda l, j: (l, 0, j)),
            pl.BlockSpec((1, 1, tn), lambda l, j: (l, 0, j)),
        ],
        out_specs=pl.BlockSpec((1, nb, tn), lambda l, j: (l, 0, j)),
        name="ada",
        compiler_params=pltpu.CompilerParams(dimension_semantics=("arbitrary", "arbitrary")),
    )(c_all, w_ada, b_ada.reshape(depth, 1, n))


def _pre_kernel(x_ref, mod_ref, cos_ref, sin_ref, gmix_ref, win_ref, gsq_ref, gsk_ref, gqa_ref,
                wqn_ref, wqp_ref, gkva_ref, wkk_ref, wkv_ref, gqn_ref, gkn_ref, gqpe_ref, gkpe_ref,
                m64_ref, m32_ref,
                ksb_o, vsb_o, ckv_o, kpe_o,
                qsb_b, ksb_b, vsb_b, qn_b, qpe_b, kn_b, vm_b, kpe_b):
    nb, ts, d = x_ref.shape
    r = nb * ts
    x = x_ref[...]
    sh = mod_ref[:, 0:1, :]
    sc = mod_ref[:, 1:2, :]
    h = _rms(x, gmix_ref[...]) * (1.0 + sc) + sh
    u = _dot(h.reshape(r, d).astype(BF16), win_ref[...])

    m64 = m64_ref[...]
    q_sb = _seg_rms(u[:, C_QSB:C_QSB + SB_W], m64, gsq_ref[...])
    k_sb = _seg_rms(u[:, C_KSB:C_KSB + SB_W], m64, gsk_ref[...])
    v_sb = u[:, C_VSB:C_VSB + SB_W]
    ksb_o[0] = k_sb.reshape(nb, ts, SB_W)
    vsb_o[0] = v_sb.reshape(nb, ts, SB_W)
    qsb_b[...] = (q_sb * (1.0 / math.sqrt(SB_DIM))).astype(BF16).reshape(nb, ts, SB_W)
    ksb_b[...] = k_sb.astype(BF16).reshape(nb, ts, SB_W)
    vsb_b[...] = v_sb.astype(BF16).reshape(nb, ts, SB_W)

    cos = cos_ref[...]
    sin = sin_ref[...]
    lane = lax.broadcasted_iota(jnp.int32, (1, PE_W), 1)
    first_half = (lane & (MLA_ROPE - 1)) < (MLA_ROPE // 2)

    q_lat = _rms(u[:, C_QLAT:C_QLAT + Q_RANK], gqa_ref[...]).astype(BF16)
    qn = _seg_rms(_dot(q_lat, wqn_ref[...]), m64, gqn_ref[...])
    qn_b[...] = (qn * MLA_SCALE).astype(BF16).reshape(nb, ts, MLA_W)
    qp = _seg_rms(_dot(q_lat, wqp_ref[...]), m32_ref[...], gqpe_ref[...])
    qp_rot = _rot_half(qp, first_half)
    qp = qp.reshape(nb, ts, PE_W) * cos + qp_rot.reshape(nb, ts, PE_W) * sin
    qpe_b[...] = (qp * MLA_SCALE).astype(BF16)

    ckv = _rms(u[:, C_KVLAT:C_KVLAT + KV_RANK], gkva_ref[...])
    ckv_o[0] = ckv.reshape(nb, ts, KV_RANK)
    ckv16 = ckv.astype(BF16)
    kn = _seg_rms(_dot(ckv16, wkk_ref[...]), m64, gkn_ref[...])
    kn_b[...] = kn.astype(BF16).reshape(nb, ts, MLA_W)
    vm_b[...] = _dot(ckv16, wkv_ref[...]).astype(BF16).reshape(nb, ts, MLA_W)

    kp = _rms(u[:, C_KPE:C_KPE + LANES], gkpe_ref[...])
    kp_rot = _rot_half(kp, first_half[:, :LANES])
    kp = kp.reshape(nb, ts, LANES) * cos[:, :, :LANES] + kp_rot.reshape(nb, ts, LANES) * sin[:, :, :LANES]
    kpe_o[0] = kp[:, :, :MLA_ROPE]
    kpe_b[...] = kp.astype(BF16)


def _pre_kernel_aliased(*refs):
    _pre_kernel(*refs[N_STATE:])


N_STATE = 4


def _pre(x, mod, cos, sin, w, nb, ts, layer, depth, state):
    b, s, d = x.shape
    grid = (b // nb, s // ts)
    tok = lambda width: pl.BlockSpec((nb, ts, width), lambda i, j: (i, j, 0))
    slab = lambda width: pl.BlockSpec((1, nb, ts, width), lambda i, j: (layer, i, j, 0))
    const = lambda a: pl.BlockSpec(a.shape, lambda i, j: (0,) * a.ndim)
    weights = [w["g_mix"], w["w_in"], w["g_sb_q"], w["g_sb_k"], w["g_qa"], w["w_qn"], w["w_qp"], w["g_kva"],
               w["w_kk"], w["w_kv"], w["g_mq_nope"], w["g_mk_nope"], w["g_mq_pe"], w["g_mk_pe"],
               w["m64"], w["m32"]]
    state_widths = (SB_W, SB_W, KV_RANK, MLA_ROPE)
    bf16_widths = (SB_W, SB_W, SB_W, MLA_W, PE_W, MLA_W, MLA_W, LANES)
    out_shape = tuple(jax.ShapeDtypeStruct((depth, b, s, wd), F32) for wd in state_widths) + \
        tuple(jax.ShapeDtypeStruct((b, s, wd), BF16) for wd in bf16_widths)
    in_specs = [tok(d),
                pl.BlockSpec((nb, 6, d), lambda i, j: (i, 0, 0)),
                pl.BlockSpec((1, ts, PE_W), lambda i, j: (0, j, 0)),
                pl.BlockSpec((1, ts, PE_W), lambda i, j: (0, j, 0))] + [const(a) for a in weights]
    args = (x, mod, cos, sin, *weights)
    body, aliases = _pre_kernel, {}
    if state is not None:
        body, aliases = _pre_kernel_aliased, {i: i for i in range(N_STATE)}
        in_specs = [pl.BlockSpec(memory_space=pl.ANY)] * N_STATE + in_specs
        args = (*state, *args)
    outs = pl.pallas_call(
        body,
        out_shape=out_shape,
        grid=grid,
        in_specs=in_specs,
        out_specs=tuple(slab(wd) for wd in state_widths) + tuple(tok(wd) for wd in bf16_widths),
        input_output_aliases=aliases,
        name="pre",
        compiler_params=pltpu.CompilerParams(dimension_semantics=("arbitrary", "arbitrary"),
                                             vmem_limit_bytes=VMEM_LIMIT),
    )(*args)
    return outs[:N_STATE], outs[N_STATE:]


def _stick_logs(z, mask):
    sign_bit = jnp.uint32(0x80000000)
    neg_abs = lax.bitcast_convert_type(lax.bitcast_convert_type(z, jnp.uint32) | sign_bit, F32)
    log_beta = jnp.minimum(z, 0.0) - jnp.log(1.0 + jnp.exp(neg_abs))
    log_1m = log_beta - z
    if mask is not None:
        log_1m = jnp.where(mask, log_1m, 0.0)
    return log_beta, log_1m


def _head_operands(qsb_ref, qn_ref, qpe_ref, h):
    p, e = divmod(h, 2)
    lane = lax.broadcasted_iota(jnp.int32, (1, LANES), 1)
    half = (lane >= SB_DIM) if e == 1 else (lane < SB_DIM)
    sl = slice(PAIR_W * p, PAIR_W * (p + 1))
    q_sb = jnp.where(half, qsb_ref[:, sl], jnp.zeros((), BF16))
    q_n = jnp.where(half, qn_ref[:, sl], jnp.zeros((), BF16))
    slab, pos = divmod(h, KPE_TILE)
    pe_lanes = (lane >= MLA_ROPE * pos) & (lane < MLA_ROPE * (pos + 1))
    q_pe = jnp.where(pe_lanes, qpe_ref[:, LANES * slab:LANES * (slab + 1)], jnp.zeros((), BF16))
    return q_sb, jnp.concatenate([q_n, q_pe], axis=-1)


def _attend_block(q_ops, k_sb, v_sb, k_n, k_pe, v_m, tri, sb_mask, mla_mask, first,
                  acc_sb, acc_m, c_s, m_s, with_sb=True):
    heads = range(SB_HEADS)
    slab = lambda a, h: a[:, PAIR_W * (h // 2):PAIR_W * (h // 2 + 1)]
    kcat = [jnp.concatenate([k_n[:, PAIR_W * p:PAIR_W * (p + 1)], k_pe], axis=-1) for p in range(N_PAIRS)]

    z_sb = [_nt_dot(q_ops[h][0], slab(k_sb, h)) for h in heads] if with_sb else []
    z_m = [_nt_dot(q_ops[h][1], kcat[h // 2]) for h in heads]

    log_w, suffix = [], []
    for h in heads if with_sb else ():
        log_beta, log_1m = _stick_logs(z_sb[h], sb_mask)
        suffix.append(_dot(log_1m.astype(BF16), tri))
        row_sum = jnp.sum(log_1m, axis=-1, keepdims=True)
        if first:
            log_w.append(log_beta)
            c_s[h] = jnp.broadcast_to(row_sum, (log_1m.shape[0], LANES))
        else:
            carry = c_s[h]
            log_w.append(log_beta + jnp.concatenate([carry] * (log_beta.shape[1] // LANES), axis=-1))
            c_s[h] = carry + row_sum

    lane = lax.broadcasted_iota(jnp.int32, (1, LANES), 1)
    for h in heads:
        z = z_m[h]
        if mla_mask is not None:
            z = jnp.where(mla_mask, z, NEG_BIG)
        tq = z.shape[0]
        halves = [z[:, LANES * i:LANES * (i + 1)] for i in range(z.shape[1] // LANES)]
        zmax = jnp.max(functools.reduce(jnp.maximum, halves), axis=-1, keepdims=True)
        own_half = (lane >= SB_DIM) if h % 2 else (lane < SB_DIM)
        v_ones = jnp.where(own_half, slab(v_m, h), jnp.ones((), BF16))
        if first:
            m_new = jnp.broadcast_to(zmax, (tq, LANES))
        else:
            m_old = m_s[h]
            m_new = jnp.maximum(m_old, zmax)
        prob = jnp.concatenate([jnp.exp(zh - m_new) for zh in halves], axis=-1).astype(BF16)
        m_s[h] = m_new
        if first:
            acc_m[h] = _dot(prob, v_ones)
        else:
            acc_m[h] = jnp.exp(m_old - m_new) * acc_m[h] + _dot(prob, v_ones)

    for h in heads if with_sb else ():
        wgt = jnp.exp(log_w[h] + suffix[h])
        if sb_mask is not None:
            wgt = jnp.where(sb_mask, wgt, 0.0)
        pv = _dot(wgt.astype(BF16), slab(v_sb, h))
        if first:
            acc_sb[h] = pv
        else:
            acc_sb[h] += pv


def _write_out(o_ref, acc_sb, acc_m):
    lane = lax.broadcasted_iota(jnp.int32, (1, LANES), 1)
    lo_half = lane < SB_DIM
    for p in range(N_PAIRS):
        h0, h1 = 2 * p, 2 * p + 1
        o_ref[:, PAIR_W * p:PAIR_W * (p + 1)] = jnp.where(lo_half, acc_sb[h0], acc_sb[h1]).astype(o_ref.dtype)
        a0, a1 = acc_m[h0], acc_m[h1]
        mla = jnp.where(lo_half, a0 / pltpu.roll(a0, SB_DIM, axis=1), a1 / pltpu.roll(a1, SB_DIM, axis=1))
        o_ref[:, SB_W + PAIR_W * p:SB_W + PAIR_W * (p + 1)] = mla.astype(o_ref.dtype)


def _diag_masks(tq, tk, q_pos0, k_pos0):
    qpos = q_pos0 + lax.broadcasted_iota(jnp.int32, (tq, tk), 0)
    kpos = k_pos0 + lax.broadcasted_iota(jnp.int32, (tq, tk), 1)
    shift = CHUNK.bit_length() - 1
    return kpos < qpos, (kpos >> shift) <= (qpos >> shift)


def _attn_prompt_kernel(qsb_ref, qn_ref, qpe_ref, ksb_ref, vsb_ref, kn_ref, kpe_ref, vm_ref, tri_ref,
                        o_ref, acc_sb, acc_m, c_s, m_s):
    t = qsb_ref.shape[1]
    qi = pl.program_id(1)
    q_ops = [_head_operands(qsb_ref.at[0], qn_ref.at[0], qpe_ref.at[0], h) for h in range(SB_HEADS)]
    tri = tri_ref[...]

    def key_block(start):
        rows = pl.ds(start, t)
        return (ksb_ref[0, rows, :], vsb_ref[0, rows, :], kn_ref[0, rows, :], kpe_ref[0, rows, :],
                vm_ref[0, rows, :])

    sb_mask, mla_mask = _diag_masks(t, t, 0, 0)
    k_sb, v_sb, k_n, k_pe, v_m = key_block(pl.multiple_of(qi * t, t))
    _attend_block(q_ops, k_sb, v_sb, k_n, k_pe, v_m, tri, sb_mask, mla_mask, True,
                  acc_sb, acc_m, c_s, m_s)

    def earlier_block(j, with_sb):
        kb = qi - 1 - j
        k_sb, v_sb, k_n, k_pe, v_m = key_block(pl.multiple_of(kb * t, t))
        _attend_block(q_ops, k_sb, v_sb, k_n, k_pe, v_m, tri, None, None, False,
                      acc_sb, acc_m, c_s, m_s, with_sb=with_sb)

    def sb_alive():
        return (jnp.max(c_s[...]) > SB_DEAD).astype(jnp.int32)

    def sb_cond(state):
        j, alive = state
        return jnp.logical_and(j < qi, alive > 0)

    def sb_body(state):
        j, _ = state
        earlier_block(j, True)
        return j + 1, sb_alive()

    j_done, _ = lax.while_loop(sb_cond, sb_body, (jnp.int32(0), sb_alive()))

    def mla_body(j, _):
        earlier_block(j, False)
        return 0

    lax.fori_loop(j_done, qi, mla_body, 0)
    _write_out(o_ref.at[0], acc_sb, acc_m)


def _attn_prompt(qsb, qn, qpe, ksb, vsb, kn, kpe, vm, tri):
    b, s, _ = qsb.shape
    t = tri.shape[0]
    qblk = lambda width: pl.BlockSpec((1, t, width), lambda i, j: (i, j, 0))
    full = lambda width: pl.BlockSpec((1, s, width), lambda i, j: (i, 0, 0))
    return pl.pallas_call(
        _attn_prompt_kernel,
        out_shape=jax.ShapeDtypeStruct((b, s, SB_W + MLA_W), BF16),
        grid=(b, s // t),
        in_specs=[qblk(SB_W), qblk(MLA_W), qblk(PE_W), full(SB_W), full(SB_W), full(MLA_W), full(LANES),
                  full(MLA_W), pl.BlockSpec((t, t), lambda i, j: (0, 0))],
        out_specs=qblk(SB_W + MLA_W),
        scratch_shapes=[pltpu.VMEM((SB_HEADS, t, LANES), F32)] * 4,
        name="attn_prompt",
        compiler_params=pltpu.CompilerParams(dimension_semantics=("arbitrary", "arbitrary"),
                                             vmem_limit_bytes=VMEM_LIMIT),
    )(qsb, qn, qpe, ksb, vsb, kn, kpe, vm, tri)


def _tn_dot(a, b):
    return lax.dot_general(a, b, (((0,), (0,)), ((), ())), preferred_element_type=F32)


def _sample_block(k_sb, v_sb, k_cat, v_m, tri_u, sb_mask, mla_mask, first,
                  qbd_sb, qbd_m, acc_sb, acc_m, c_s, m_s, l_s):
    z = _dot(k_sb, qbd_sb[...])
    log_beta, log_1m = _stick_logs(z, sb_mask)
    suffix = _dot(tri_u, log_1m.astype(BF16))
    col_sum = jnp.sum(log_1m, axis=0, keepdims=True)
    if first:
        wgt = jnp.exp(log_beta + suffix)
        c_s[...] = col_sum
    else:
        carry = c_s[...]
        wgt = jnp.exp(log_beta + suffix + carry)
        c_s[...] = carry + col_sum
    if sb_mask is not None:
        wgt = jnp.where(sb_mask, wgt, 0.0)
    pv = _tn_dot(v_sb, wgt.astype(BF16))
    if first:
        acc_sb[...] = pv
    else:
        acc_sb[...] += pv

    z = _dot(k_cat, qbd_m[...])
    if mla_mask is not None:
        z = jnp.where(mla_mask, z, NEG_BIG)
    zmax = jnp.max(z, axis=0, keepdims=True)
    if first:
        prob = jnp.exp(z - zmax)
        m_s[...] = zmax
        l_s[...] = jnp.sum(prob, axis=0, keepdims=True)
        acc_m[...] = _tn_dot(v_m, prob.astype(BF16))
    else:
        m_old = m_s[...]
        m_new = jnp.maximum(m_old, zmax)
        prob = jnp.exp(z - m_new)
        alpha = jnp.exp(m_old - m_new)
        m_s[...] = m_new
        l_s[...] = alpha * l_s[...] + jnp.sum(prob, axis=0, keepdims=True)
        acc_m[...] = alpha * acc_m[...] + _tn_dot(v_m, prob.astype(BF16))


def _attn_sample_kernel(qsb_ref, qn_ref, qpe_ref, ksb_ref, vsb_ref, kn_ref, kpe_ref, vm_ref,
                        cksb_ref, cvsb_ref, cckv_ref, ckpe_ref, wkk_ref, wkv_ref, gkn_ref, m64_ref, rep_ref,
                        tri_ref, sel_ref,
                        o_ref, qbd_sb, qbd_m, acc_sb, acc_m, c_s, m_s, l_s, *, past_len):
    tq = qsb_ref.shape[1]
    ncol = SB_HEADS * tq
    q_shift = tq.bit_length() - 1
    j = pl.program_id(1)
    state = (qbd_sb, qbd_m, acc_sb, acc_m, c_s, m_s, l_s)

    @pl.when(j == 0)
    def _():
        sel = sel_ref[...]
        row_head = lax.broadcasted_iota(jnp.int32, (SB_W, ncol), 0) >> (SB_DIM.bit_length() - 1)
        col_head = lax.broadcasted_iota(jnp.int32, (SB_W, ncol), 1) >> q_shift
        on_diag = row_head == col_head
        qbd_sb[...] = jnp.where(on_diag, _tn_dot(qsb_ref[0], sel), 0.0).astype(BF16)
        qbd_m[0:MLA_W, :] = jnp.where(on_diag, _tn_dot(qn_ref[0], sel), 0.0).astype(BF16)
        pe_t = _tn_dot(qpe_ref[0], sel)
        head_of_col = lax.broadcasted_iota(jnp.int32, (MLA_ROPE, ncol), 1) >> q_shift
        pe = jnp.zeros((MLA_ROPE, ncol), F32)
        for h in range(MLA_HEADS):
            pe = pe + jnp.where(head_of_col == h, pe_t[MLA_ROPE * h:MLA_ROPE * (h + 1), :], 0.0)
        qbd_m[MLA_W:MLA_W + LANES, :] = jnp.concatenate(
            [pe, jnp.zeros((LANES - MLA_ROPE, ncol), F32)], axis=0).astype(BF16)

        key_i = lax.broadcasted_iota(jnp.int32, (tq, ncol), 0)
        qry_i = lax.broadcasted_iota(jnp.int32, (tq, ncol), 1) & (tq - 1)
        c_shift = CHUNK.bit_length() - 1
        sb_mask = key_i < qry_i
        mla_mask = ((past_len + key_i) >> c_shift) <= ((past_len + qry_i) >> c_shift)
        k_cat = jnp.concatenate([kn_ref[0], kpe_ref[0]], axis=-1)
        _sample_block(ksb_ref[0], vsb_ref[0], k_cat, vm_ref[0], tri_ref[0:tq, 0:tq], sb_mask, mla_mask, True,
                      *state)

    ckv16 = cckv_ref[0, 0].astype(BF16)
    c_kn = _seg_rms(_dot(ckv16, wkk_ref[...]), m64_ref[...], gkn_ref[...]).astype(BF16)
    c_vm = _dot(ckv16, wkv_ref[...]).astype(BF16)
    c_kpe = _dot(ckpe_ref[0, 0].astype(BF16), rep_ref[...]).astype(BF16)
    k_cat = jnp.concatenate([c_kn, c_kpe], axis=-1)
    _sample_block(cksb_ref[0, 0].astype(BF16), cvsb_ref[0, 0].astype(BF16), k_cat, c_vm, tri_ref[...],
                  None, None, False, *state)

    @pl.when(j == pl.num_programs(1) - 1)
    def _():
        lane_head = lax.broadcasted_iota(jnp.int32, (tq, SB_W), 1) >> (SB_DIM.bit_length() - 1)

        def diag_blocks(acc_t):
            acc = acc_t.T
            out = jnp.zeros((tq, SB_W), F32)
            for h in range(SB_HEADS):
                out = out + jnp.where(lane_head == h, acc[tq * h:tq * (h + 1), :], 0.0)
            return out

        o_ref[0, :, 0:SB_W] = diag_blocks(acc_sb[...]).astype(o_ref.dtype)
        o_ref[0, :, SB_W:SB_W + MLA_W] = diag_blocks(acc_m[...] / l_s[...]).astype(o_ref.dtype)


def _attn_sample(layer, qsb, qn, qpe, ksb, vsb, kn, kpe, vm, cache_k, cache_v, cache_ckv, cache_kpe, w, tri_u, sel):
    b, tq, _ = qsb.shape
    weights = [w["w_kk"], w["w_kv"], w["g_mk_nope"], w["m64"], w["rep"]]
    past_len = cache_k.shape[2]
    tk = tri_u.shape[0]
    nkb = past_len // tk
    ncol = SB_HEADS * tq
    new = lambda width: pl.BlockSpec((1, tq, width), lambda i, j: (i, 0, 0))
    rev_l = lambda width: pl.BlockSpec((1, 1, tk, width), lambda i, j: (layer, i, nkb - 1 - j, 0))
    const = lambda a: pl.BlockSpec(a.shape, lambda i, j: (0,) * a.ndim)
    return pl.pallas_call(
        functools.partial(_attn_sample_kernel, past_len=past_len),
        out_shape=jax.ShapeDtypeStruct((b, tq, SB_W + MLA_W), BF16),
        grid=(b, nkb),
        in_specs=[new(SB_W), new(MLA_W), new(PE_W), new(SB_W), new(SB_W), new(MLA_W), new(LANES), new(MLA_W),
                  rev_l(SB_W), rev_l(SB_W), rev_l(KV_RANK), rev_l(MLA_ROPE)] + [const(a) for a in weights] +
                 [const(tri_u), const(sel)],
        out_specs=new(SB_W + MLA_W),
        scratch_shapes=[pltpu.VMEM((SB_W, ncol), BF16), pltpu.VMEM((MLA_W + LANES, ncol), BF16),
                        pltpu.VMEM((SB_W, ncol), F32), pltpu.VMEM((MLA_W, ncol), F32),
                        pltpu.VMEM((1, ncol), F32), pltpu.VMEM((1, ncol), F32), pltpu.VMEM((1, ncol), F32)],
        name="attn_sample",
        compiler_params=pltpu.CompilerParams(dimension_semantics=("arbitrary", "arbitrary"),
                                             vmem_limit_bytes=VMEM_LIMIT),
    )(qsb, qn, qpe, ksb, vsb, kn, kpe, vm, cache_k, cache_v, cache_ckv, cache_kpe, *weights, tri_u, sel)


def _post_kernel(x_ref, o_ref, mod_ref, wout_ref, gffn_ref, wr_ref, br_ref, ltri_ref, wg_ref, wu_ref, wd_ref,
                 out_ref, hs_s, gs_s, perm_s, y_s, meta_s):
    nb, ts, d = x_ref.shape
    r = nb * ts
    e = pl.program_id(2)
    group_shift = EXP_PER_GROUP.bit_length() - 1

    @pl.when(e == 0)
    def _():
        gt_a = mod_ref[:, 2:3, :]
        sh_m = mod_ref[:, 3:4, :]
        sc_m = mod_ref[:, 4:5, :]
        att = _dot(o_ref[...].reshape(r, o_ref.shape[-1]), wout_ref[...])
        x1 = x_ref[...] + gt_a * att.reshape(nb, ts, d)
        out_ref[...] = x1
        h2 = (_rms(x1, gffn_ref[...]) * (1.0 + sc_m) + sh_m).reshape(r, d).astype(BF16)
        logits = _dot(h2, wr_ref[...]) + br_ref[...]
        lane_i = lax.broadcasted_iota(jnp.int32, (r, LANES), 1)
        lane = lane_i.astype(F32)
        group_of_lane = (lane_i >> group_shift).astype(F32)
        neg_inf = jnp.float32(-jnp.inf)
        no_lane = jnp.float32(LANES)
        gl = jnp.where(lane_i < N_GROUPS, logits[:, LANES:], neg_inf)
        gmax = jnp.max(gl, axis=-1, keepdims=True)
        gsel = jnp.min(jnp.where(gl == gmax, lane, no_lane), axis=-1, keepdims=True)
        p_g = 1.0 / jnp.sum(jnp.exp(gl - gmax), axis=-1, keepdims=True)
        in_group = group_of_lane == gsel
        em = jnp.where(in_group, logits[:, :LANES], neg_inf)
        v1 = jnp.max(em, axis=-1, keepdims=True)
        i1 = jnp.min(jnp.where(em == v1, lane, no_lane), axis=-1, keepdims=True)
        em2 = jnp.where(lane == i1, neg_inf, em)
        v2 = jnp.max(em2, axis=-1, keepdims=True)
        i2 = jnp.min(jnp.where(em2 == v2, lane, no_lane), axis=-1, keepdims=True)
        t = jnp.exp(v2 - v1)
        w1 = p_g / (1.0 + t)
        first_of_group = gsel * float(EXP_PER_GROUP)
        gates = jnp.where(lane == i1 - first_of_group, w1, 0.0) + jnp.where(lane == i2 - first_of_group, w1 * t, 0.0)

        in_g = jnp.where(lane == gsel, 1.0, 0.0)
        count = jnp.sum(in_g, axis=0, keepdims=True)
        earlier = _dot(ltri_ref[...], in_g.astype(BF16))
        lane_row = lax.broadcasted_iota(jnp.int32, (1, LANES), 1)
        start = jnp.int32(0)
        start_of = jnp.zeros((1, LANES), F32)
        for g in range(N_GROUPS):
            n_g = jnp.sum(jnp.where(lane_row == g, count, 0.0)).astype(jnp.int32)
            n_tiles = (n_g + (GROUP_ALIGN - 1)) // GROUP_ALIGN
            meta_s[g] = start
            meta_s[N_GROUPS + g] = n_tiles
            start_of = jnp.where(lane_row == g, start.astype(F32), start_of)
            start = start + n_tiles * GROUP_ALIGN
        pos = jnp.sum(in_g * (start_of + earlier), axis=-1, keepdims=True)
        pos_row = jnp.broadcast_to(pos, (r, LANES)).T[0:1, :]
        g_hi = gates.astype(BF16)
        g_lo = (gates - g_hi.astype(F32)).astype(BF16)
        payload = jnp.concatenate([h2, g_hi, g_lo], axis=-1)
        n_rows = perm_s.shape[0]
        chunk = 256
        for c in range(n_rows // chunk):
            rows = slice(c * chunk, (c + 1) * chunk)
            row = (c * chunk + lax.broadcasted_iota(jnp.int32, (chunk, r), 0)).astype(F32)
            perm = jnp.where(row == pos_row, 1.0, 0.0).astype(BF16)
            perm_s[rows, :] = perm
            moved = _dot(perm, payload)
            hs_s[rows, :] = moved[:, :d].astype(BF16)
            gs_s[rows, :] = moved[:, d:d + LANES] + moved[:, d + LANES:]
        y_s[...] = jnp.zeros(y_s.shape, F32)

    first_expert = e * EXPERTS_PER_STEP
    g = first_expert >> group_shift
    j0 = first_expert & (EXP_PER_GROUP - 1)
    bucket_start = meta_s[g]
    lane = lax.broadcasted_iota(jnp.int32, (GROUP_ALIGN, LANES), 1)

    def row_tile(i, _):
        rows = pl.ds(pl.multiple_of(bucket_start + i * GROUP_ALIGN, GROUP_ALIGN), GROUP_ALIGN)
        xs = hs_s[rows, :]
        gs = gs_s[rows, :]
        y = None
        for k in range(EXPERTS_PER_STEP):
            a = _dot(xs, wg_ref[k])
            bb = _dot(xs, wu_ref[k])
            g_e = jnp.sum(jnp.where(lane == j0 + k, gs, 0.0), axis=-1, keepdims=True)
            act = (a / (1.0 + jnp.exp(-a))) * bb * g_e
            contrib = _dot(act.astype(BF16), wd_ref[k])
            y = contrib if y is None else y + contrib
        y_s[rows, :] += y
        return 0

    lax.fori_loop(0, meta_s[N_GROUPS + g], row_tile, 0)

    @pl.when(e == pl.num_programs(2) - 1)
    def _():
        gt_m = mod_ref[:, 5:6, :]
        y = _tn_dot(perm_s[...], y_s[...].astype(BF16))
        out_ref[...] += gt_m * y.reshape(nb, ts, d)


def _post(x, o, mod, w, nb, ts):
    b, s, d = x.shape
    r = nb * ts
    n_rows = r + N_GROUPS * GROUP_ALIGN
    n_rows = -(-n_rows // 256) * 256
    ltri = (jnp.arange(r)[:, None] > jnp.arange(r)[None, :]).astype(BF16)
    tok = lambda width: pl.BlockSpec((nb, ts, width), lambda i, j, e: (i, j, 0))
    const = lambda a: pl.BlockSpec(a.shape, lambda i, j, e: (0,) * a.ndim, pipeline_mode=pl.Buffered(1))
    per_e = lambda a: pl.BlockSpec((EXPERTS_PER_STEP,) + a.shape[1:], lambda i, j, e: (e, 0, 0))
    return pl.pallas_call(
        _post_kernel,
        out_shape=jax.ShapeDtypeStruct((b, s, d), F32),
        grid=(b // nb, s // ts, N_EXPERTS // EXPERTS_PER_STEP),
        in_specs=[pl.BlockSpec((nb, ts, d), lambda i, j, e: (i, j, 0), pipeline_mode=pl.Buffered(1)),
                  pl.BlockSpec((nb, ts, o.shape[-1]), lambda i, j, e: (i, j, 0), pipeline_mode=pl.Buffered(1)),
                  pl.BlockSpec((nb, 6, d), lambda i, j, e: (i, 0, 0)),
                  const(w["w_out"]), const(w["g_ffn"]), const(w["w_r"]), const(w["b_r"]), const(ltri),
                  per_e(w["w_gate"]), per_e(w["w_up"]), per_e(w["w_down"])],
        out_specs=tok(d),
        scratch_shapes=[pltpu.VMEM((n_rows, d), BF16), pltpu.VMEM((n_rows, LANES), F32),
                        pltpu.VMEM((n_rows, r), BF16), pltpu.VMEM((n_rows, d), F32),
                        pltpu.SMEM((2 * N_GROUPS,), jnp.int32)],
        name="post",
        compiler_params=pltpu.CompilerParams(dimension_semantics=("arbitrary", "arbitrary", "arbitrary"),
                                             vmem_limit_bytes=VMEM_LIMIT),
    )(x, o, mod, w["w_out"], w["g_ffn"], w["w_r"], w["b_r"], ltri, w["w_gate"], w["w_up"], w["w_down"])


def _block_mean(width, seg):
    idx = jnp.arange(width) // seg
    return jnp.where(idx[:, None] == idx[None, :], 1.0 / seg, 0.0).astype(BF16)


def _layer_weights(l, g_mix, w_in, g_sb_q, g_sb_k, g_qa, w_qb, g_kva, w_kvb, g_mq_nope, g_mk_nope, g_mq_pe,
                   g_mk_pe, w_out, g_ffn, w_rg, b_rg, w_re, b_re, w_gate, w_up, w_down):
    d = w_in.shape[1]
    row = lambda g, reps: jnp.tile(g[l], reps).reshape(1, -1)
    kpe_cols = jnp.tile(w_in[l][:, C_KPE:C_KPE + MLA_ROPE], (1, KPE_TILE))
    w_qb3 = w_qb[l].reshape(Q_RANK, MLA_HEADS, MLA_NOPE + MLA_ROPE)
    w_kvb3 = w_kvb[l].reshape(KV_RANK, MLA_HEADS, MLA_NOPE + MLA_V)
    pad = jnp.zeros((d, LANES), F32)
    w_r = jnp.concatenate([pad.at[:, :N_EXPERTS].set(w_re[l]), pad.at[:, :N_GROUPS].set(w_rg[l])], axis=1)
    zrow = jnp.zeros((LANES,), F32)
    b_r = jnp.concatenate([zrow.at[:N_EXPERTS].set(b_re[l]), zrow.at[:N_GROUPS].set(b_rg[l])]).reshape(1, -1)
    rep = (jnp.arange(MLA_ROPE)[:, None] == (jnp.arange(LANES)[None, :] % MLA_ROPE)).astype(BF16)
    return dict(
        g_mix=row(g_mix, 1),
        w_in=jnp.concatenate([w_in[l][:, :C_KPE], kpe_cols], axis=1).astype(BF16),
        g_sb_q=row(g_sb_q, SB_HEADS), g_sb_k=row(g_sb_k, SB_HEADS), g_qa=row(g_qa, 1),
        w_qn=w_qb3[:, :, :MLA_NOPE].reshape(Q_RANK, MLA_W).astype(BF16),
        w_qp=w_qb3[:, :, MLA_NOPE:].reshape(Q_RANK, PE_W).astype(BF16),
        g_kva=row(g_kva, 1),
        w_kk=w_kvb3[:, :, :MLA_NOPE].reshape(KV_RANK, MLA_W).astype(BF16),
        w_kv=w_kvb3[:, :, MLA_NOPE:].reshape(KV_RANK, MLA_W).astype(BF16),
        g_mq_nope=row(g_mq_nope, MLA_HEADS), g_mk_nope=row(g_mk_nope, MLA_HEADS),
        g_mq_pe=row(g_mq_pe, MLA_HEADS), g_mk_pe=row(g_mk_pe, KPE_TILE),
        m64=_block_mean(SB_W, SB_DIM), m32=_block_mean(PE_W, MLA_ROPE), rep=rep,
        w_out=w_out[l].astype(BF16), g_ffn=row(g_ffn, 1), w_r=w_r.astype(BF16), b_r=b_r,
        w_gate=w_gate[l].astype(BF16), w_up=w_up[l].astype(BF16), w_down=w_down[l].astype(BF16),
    )


def _rope_tables(pos):
    inv = ROPE_THETA ** (-jnp.arange(0, MLA_ROPE, 2, dtype=F32) / MLA_ROPE)
    ang = pos.astype(F32)[:, None] * inv[None, :]
    cos, sin = jnp.cos(ang), jnp.sin(ang)
    cos = jnp.tile(jnp.concatenate([cos, cos], axis=-1), (1, MLA_HEADS))
    sin = jnp.tile(jnp.concatenate([-sin, sin], axis=-1), (1, MLA_HEADS))
    return cos[None], sin[None]


def _largest_tile(n, cap):
    t = min(n, cap)
    while n % t:
        t -= 8
    return t


def kernel(x_prompt, x_sample, cache_sb_k, cache_sb_v, cache_mla_ckv, cache_mla_kpe, c_prompt, c_sample, w_ada, b_ada, g_mix, w_in, g_sb_q, g_sb_k, g_qa, w_qb, g_kva, w_kvb, g_mq_nope, g_mk_nope, g_mq_pe, g_mk_pe, w_out, g_ffn, w_rg, b_rg, w_re, b_re, w_gate, w_up, w_down):
    bp, sp, d = x_prompt.shape
    bs, ss, _ = x_sample.shape
    depth = w_ada.shape[0]
    past_len = cache_sb_k.shape[2]
    t_attn = 256
    t_cache = 512
    assert sp % t_attn == 0 and past_len % t_cache == 0 and ss % 8 == 0 and ss & (ss - 1) == 0

    mod = _ada(jnp.concatenate([c_prompt, c_sample], axis=0), w_ada, b_ada).reshape(depth, bp + bs, 6, d)
    cos_p, sin_p = _rope_tables(jnp.arange(sp, dtype=jnp.int32))
    cos_s, sin_s = _rope_tables(past_len + jnp.arange(ss, dtype=jnp.int32))
    tri = (jnp.arange(t_attn)[:, None] > jnp.arange(t_attn)[None, :]).astype(BF16)
    tri_u = (jnp.arange(t_cache)[:, None] < jnp.arange(t_cache)[None, :]).astype(BF16)
    sel = (jnp.arange(ss)[:, None] == (jnp.arange(SB_HEADS * ss)[None, :] % ss)).astype(BF16)
    cache_k = cache_sb_k.reshape(depth, bs, past_len, SB_W)
    cache_v = cache_sb_v.reshape(depth, bs, past_len, SB_W)

    ts_p = _largest_tile(sp, 512)
    ts_post = _largest_tile(sp, 1024)
    yp, ys = x_prompt, x_sample
    st_p, st_s = None, None
    for l in range(depth):
        w = _layer_weights(l, g_mix, w_in, g_sb_q, g_sb_k, g_qa, w_qb, g_kva, w_kvb, g_mq_nope, g_mk_nope,
                           g_mq_pe, g_mk_pe, w_out, g_ffn, w_rg, b_rg, w_re, b_re, w_gate, w_up, w_down)
        mod_p, mod_s = mod[l, :bp], mod[l, bp:]

        st_p, (qsb, ksb, vsb, qn, qpe, kn, vm, kpe) = _pre(yp, mod_p, cos_p, sin_p, w, 1, ts_p, l, depth, st_p)
        o = _attn_prompt(qsb, qn, qpe, ksb, vsb, kn, kpe, vm, tri)
        yp = _post(yp, o, mod_p, w, 1, ts_post)

        st_s, (qsb, ksb, vsb, qn, qpe, kn, vm, kpe) = _pre(ys, mod_s, cos_s, sin_s, w, bs, ss, l, depth, st_s)
        o = _attn_sample(l, qsb, qn, qpe, ksb, vsb, kn, kpe, vm, cache_k, cache_v, cache_mla_ckv, cache_mla_kpe,
                         w, tri_u, sel)
        ys = _post(ys, o, mod_s, w, bs, ss)

    heads = lambda a: a.reshape(*a.shape[:3], SB_HEADS, SB_DIM)
    return (yp, ys, heads(st_p[0]), heads(st_p[1]), st_p[2], st_p[3],
            heads(st_s[0]), heads(st_s[1]), st_s[2], st_s[3])
```

```python
import functools
import math

import jax
import jax.numpy as jnp
from jax import lax
from jax.experimental import pallas as pl
from jax.experimental.pallas import tpu as pltpu

F32 = jnp.float32
BF16 = jnp.bfloat16

EPS = 1e-6
CHUNK = 64
SB_HEADS = 8
SB_DIM = 64
SB_W = SB_HEADS * SB_DIM
MLA_HEADS = 8
MLA_NOPE = 64
MLA_ROPE = 32
MLA_V = 64
Q_RANK = 384
KV_RANK = 256
ROPE_THETA = 10000.0
MLA_W = MLA_HEADS * MLA_V
N_GROUPS = 4
EXP_PER_GROUP = 8
N_EXPERTS = N_GROUPS * EXP_PER_GROUP
D_EXPERT = 256

LANES = 128
PAIR_W = 2 * SB_DIM
N_PAIRS = SB_HEADS // 2
KPE_TILE = LANES // MLA_ROPE
PE_W = MLA_HEADS * MLA_ROPE
C_QSB, C_KSB, C_VSB = 0, SB_W, 2 * SB_W
C_QLAT = 3 * SB_W
C_KVLAT = C_QLAT + Q_RANK
C_KPE = C_KVLAT + KV_RANK
IN_COLS_PAD = C_KPE + LANES
MLA_SCALE = 1.0 / math.sqrt(MLA_NOPE + MLA_ROPE)
NEG_BIG = -1e30
EXPERTS_PER_STEP = 4
GROUP_ALIGN = 256
SB_DEAD = -104.0
VMEM_LIMIT = 56 * 1024 * 1024


def _nt_dot(a, b):
    return lax.dot_general(a, b, (((1,), (1,)), ((), ())), preferred_element_type=F32)


def _dot(a, b):
    return jnp.dot(a, b, preferred_element_type=F32)


def _rms(x, g):
    ms = jnp.mean(x * x, axis=-1, keepdims=True)
    return x * lax.rsqrt(ms + EPS) * g


def _seg_rms(x, seg_mean, g):
    ms = _dot((x * x).astype(BF16), seg_mean)
    return x * lax.rsqrt(ms + EPS) * g


def _rot_half(x, lane_in_group_lt_half):
    w = x.shape[-1]
    fwd = pltpu.roll(x, w - MLA_ROPE // 2, axis=1)
    bwd = pltpu.roll(x, MLA_ROPE // 2, axis=1)
    return jnp.where(lane_in_group_lt_half, fwd, bwd)


def _ada_kernel(c_ref, w_ref, b_ref, o_ref):
    c = c_ref[...]
    s = c / (1.0 + jnp.exp(-c))
    o_ref[0] = _dot(s.astype(BF16), w_ref[0].astype(BF16)) + b_ref[0]


def _ada(c_all, w_ada, b_ada):
    depth, d, n = w_ada.shape
    nb = c_all.shape[0]
    tn = d
    return pl.pallas_call(
        _ada_kernel,
        out_shape=jax.ShapeDtypeStruct((depth, nb, n), F32),
        grid=(depth, n // tn),
        in_specs=[
            pl.BlockSpec((nb, d), lambda l, j: (0, 0)),
            pl.BlockSpec((1, d, tn), lambda l, j: (l, 0, j)),
            pl.BlockSpec((1, 1, tn), lambda l, j: (l, 0, j)),
        ],
        out_specs=pl.BlockSpec((1, nb, tn), lambda l, j: (l, 0, j)),
        name="ada",
        compiler_params=pltpu.CompilerParams(dimension_semantics=("arbitrary", "arbitrary")),
    )(c_all, w_ada, b_ada.reshape(depth, 1, n))


def _pre_kernel(x_ref, mod_ref, cos_ref, sin_ref, gmix_ref, win_ref, gsq_ref, gsk_ref, gqa_ref,
                wqn_ref, wqp_ref, gkva_ref, wkk_ref, wkv_ref, gqn_ref, gkn_ref, gqpe_ref, gkpe_ref,
                m64_ref, m32_ref,
                ksb_o, vsb_o, ckv_o, kpe_o,
                qsb_b, ksb_b, vsb_b, qn_b, qpe_b, kn_b, vm_b, kpe_b):
    nb, ts, d = x_ref.shape
    r = nb * ts
    x = x_ref[...]
    sh = mod_ref[:, 0:1, :]
    sc = mod_ref[:, 1:2, :]
    h = _rms(x, gmix_ref[...]) * (1.0 + sc) + sh
    u = _dot(h.reshape(r, d).astype(BF16), win_ref[...])

    m64 = m64_ref[...]
    q_sb = _seg_rms(u[:, C_QSB:C_QSB + SB_W], m64, gsq_ref[...])
    k_sb = _seg_rms(u[:, C_KSB:C_KSB + SB_W], m64, gsk_ref[...])
    v_sb = u[:, C_VSB:C_VSB + SB_W]
    ksb_o[0] = k_sb.reshape(nb, ts, SB_W)
    vsb_o[0] = v_sb.reshape(nb, ts, SB_W)
    qsb_b[...] = (q_sb * (1.0 / math.sqrt(SB_DIM))).astype(BF16).reshape(nb, ts, SB_W)
    ksb_b[...] = k_sb.astype(BF16).reshape(nb, ts, SB_W)
    vsb_b[...] = v_sb.astype(BF16).reshape(nb, ts, SB_W)

    cos = cos_ref[...]
    sin = sin_ref[...]
    lane = lax.broadcasted_iota(jnp.int32, (1, PE_W), 1)
    first_half = (lane & (MLA_ROPE - 1)) < (MLA_ROPE // 2)

    q_lat = _rms(u[:, C_QLAT:C_QLAT + Q_RANK], gqa_ref[...]).astype(BF16)
    qn = _seg_rms(_dot(q_lat, wqn_ref[...]), m64, gqn_ref[...])
    qn_b[...] = (qn * MLA_SCALE).astype(BF16).reshape(nb, ts, MLA_W)
    qp = _seg_rms(_dot(q_lat, wqp_ref[...]), m32_ref[...], gqpe_ref[...])
    qp_rot = _rot_half(qp, first_half)
    qp = qp.reshape(nb, ts, PE_W) * cos + qp_rot.reshape(nb, ts, PE_W) * sin
    qpe_b[...] = (qp * MLA_SCALE).astype(BF16)

    ckv = _rms(u[:, C_KVLAT:C_KVLAT + KV_RANK], gkva_ref[...])
    ckv_o[0] = ckv.reshape(nb, ts, KV_RANK)
    ckv16 = ckv.astype(BF16)
    kn = _seg_rms(_dot(ckv16, wkk_ref[...]), m64, gkn_ref[...])
    kn_b[...] = kn.astype(BF16).reshape(nb, ts, MLA_W)
    vm_b[...] = _dot(ckv16, wkv_ref[...]).astype(BF16).reshape(nb, ts, MLA_W)

    kp = _rms(u[:, C_KPE:C_KPE + LANES], gkpe_ref[...])
    kp_rot = _rot_half(kp, first_half[:, :LANES])
    kp = kp.reshape(nb, ts, LANES) * cos[:, :, :LANES] + kp_rot.reshape(nb, ts, LANES) * sin[:, :, :LANES]
    kpe_o[0] = kp[:, :, :MLA_ROPE]
    kpe_b[...] = kp.astype(BF16)


def _pre_kernel_aliased(*refs):
    _pre_kernel(*refs[N_STATE:])


N_STATE = 4


def _pre(x, mod, cos, sin, w, nb, ts, layer, depth, state):
    b, s, d = x.shape
    grid = (b // nb, s // ts)
    tok = lambda width: pl.BlockSpec((nb, ts, width), lambda i, j: (i, j, 0))
    slab = lambda width: pl.BlockSpec((1, nb, ts, width), lambda i, j: (layer, i, j, 0))
    const = lambda a: pl.BlockSpec(a.shape, lambda i, j: (0,) * a.ndim)
    weights = [w["g_mix"], w["w_in"], w["g_sb_q"], w["g_sb_k"], w["g_qa"], w["w_qn"], w["w_qp"], w["g_kva"],
               w["w_kk"], w["w_kv"], w["g_mq_nope"], w["g_mk_nope"], w["g_mq_pe"], w["g_mk_pe"],
               w["m64"], w["m32"]]
    state_widths = (SB_W, SB_W, KV_RANK, MLA_ROPE)
    bf16_widths = (SB_W, SB_W, SB_W, MLA_W, PE_W, MLA_W, MLA_W, LANES)
    out_shape = tuple(jax.ShapeDtypeStruct((depth, b, s, wd), F32) for wd in state_widths) + \
        tuple(jax.ShapeDtypeStruct((b, s, wd), BF16) for wd in bf16_widths)
    in_specs = [tok(d),
                pl.BlockSpec((nb, 6, d), lambda i, j: (i, 0, 0)),
                pl.BlockSpec((1, ts, PE_W), lambda i, j: (0, j, 0)),
                pl.BlockSpec((1, ts, PE_W), lambda i, j: (0, j, 0))] + [const(a) for a in weights]
    args = (x, mod, cos, sin, *weights)
    body, aliases = _pre_kernel, {}
    if state is not None:
        body, aliases = _pre_kernel_aliased, {i: i for i in range(N_STATE)}
        in_specs = [pl.BlockSpec(memory_space=pl.ANY)] * N_STATE + in_specs
        args = (*state, *args)
    outs = pl.pallas_call(
        body,
        out_shape=out_shape,
        grid=grid,
        in_specs=in_specs,
        out_specs=tuple(slab(wd) for wd in state_widths) + tuple(tok(wd) for wd in bf16_widths),
        input_output_aliases=aliases,
        name="pre",
        compiler_params=pltpu.CompilerParams(dimension_semantics=("arbitrary", "arbitrary"),
                                             vmem_limit_bytes=VMEM_LIMIT),
    )(*args)
    return outs[:N_STATE], outs[N_STATE:]


def _stick_logs(z, mask):
    sign_bit = jnp.uint32(0x80000000)
    neg_abs = lax.bitcast_convert_type(lax.bitcast_convert_type(z, jnp.uint32) | sign_bit, F32)
    log_beta = jnp.minimum(z, 0.0) - jnp.log(1.0 + jnp.exp(neg_abs))
    log_1m = log_beta - z
    if mask is not None:
        log_1m = jnp.where(mask, log_1m, 0.0)
    return log_beta, log_1m


def _head_operands(qsb_ref, qn_ref, qpe_ref, h):
    p, e = divmod(h, 2)
    lane = lax.broadcasted_iota(jnp.int32, (1, LANES), 1)
    half = (lane >= SB_DIM) if e == 1 else (lane < SB_DIM)
    sl = slice(PAIR_W * p, PAIR_W * (p + 1))
    q_sb = jnp.where(half, qsb_ref[:, sl], jnp.zeros((), BF16))
    q_n = jnp.where(half, qn_ref[:, sl], jnp.zeros((), BF16))
    slab, pos = divmod(h, KPE_TILE)
    pe_lanes = (lane >= MLA_ROPE * pos) & (lane < MLA_ROPE * (pos + 1))
    q_pe = jnp.where(pe_lanes, qpe_ref[:, LANES * slab:LANES * (slab + 1)], jnp.zeros((), BF16))
    return q_sb, jnp.concatenate([q_n, q_pe], axis=-1)


def _attend_block(q_ops, k_sb, v_sb, k_n, k_pe, v_m, tri, sb_mask, mla_mask, first,
                  acc_sb, acc_m, c_s, m_s, with_sb=True):
    heads = range(SB_HEADS)
    slab = lambda a, h: a[:, PAIR_W * (h // 2):PAIR_W * (h // 2 + 1)]
    kcat = [jnp.concatenate([k_n[:, PAIR_W * p:PAIR_W * (p + 1)], k_pe], axis=-1) for p in range(N_PAIRS)]

    z_sb = [_nt_dot(q_ops[h][0], slab(k_sb, h)) for h in heads] if with_sb else []
    z_m = [_nt_dot(q_ops[h][1], kcat[h // 2]) for h in heads]

    log_w, suffix = [], []
    for h in heads if with_sb else ():
        log_beta, log_1m = _stick_logs(z_sb[h], sb_mask)
        suffix.append(_dot(log_1m.astype(BF16), tri))
        row_sum = jnp.sum(log_1m, axis=-1, keepdims=True)
        if first:
            log_w.append(log_beta)
            c_s[h] = jnp.broadcast_to(row_sum, (log_1m.shape[0], LANES))
        else:
            carry = c_s[h]
            log_w.append(log_beta + jnp.concatenate([carry] * (log_beta.shape[1] // LANES), axis=-1))
            c_s[h] = carry + row_sum

    lane = lax.broadcasted_iota(jnp.int32, (1, LANES), 1)
    for h in heads:
        z = z_m[h]
        if mla_mask is not None:
            z = jnp.where(mla_mask, z, NEG_BIG)
        tq = z.shape[0]
        halves = [z[:, LANES * i:LANES * (i + 1)] for i in range(z.shape[1] // LANES)]
        zmax = jnp.max(functools.reduce(jnp.maximum, halves), axis=-1, keepdims=True)
        own_half = (lane >= SB_DIM) if h % 2 else (lane < SB_DIM)
        v_ones = jnp.where(own_half, slab(v_m, h), jnp.ones((), BF16))
        if first:
            m_new = jnp.broadcast_to(zmax, (tq, LANES))
        else:
            m_old = m_s[h]
            m_new = jnp.maximum(m_old, zmax)
        prob = jnp.concatenate([jnp.exp(zh - m_new) for zh in halves], axis=-1).astype(BF16)
        m_s[h] = m_new
        if first:
            acc_m[h] = _dot(prob, v_ones)
        else:
            acc_m[h] = jnp.exp(m_old - m_new) * acc_m[h] + _dot(prob, v_ones)

    for h in heads if with_sb else ():
        wgt = jnp.exp(log_w[h] + suffix[h])
        if sb_mask is not None:
            wgt = jnp.where(sb_mask, wgt, 0.0)
        pv = _dot(wgt.astype(BF16), slab(v_sb, h))
        if first:
            acc_sb[h] = pv
        else:
            acc_sb[h] += pv


def _write_out(o_ref, acc_sb, acc_m):
    lane = lax.broadcasted_iota(jnp.int32, (1, LANES), 1)
    lo_half = lane < SB_DIM
    for p in range(N_PAIRS):
        h0, h1 = 2 * p, 2 * p + 1
        o_ref[:, PAIR_W * p:PAIR_W * (p + 1)] = jnp.where(lo_half, acc_sb[h0], acc_sb[h1]).astype(o_ref.dtype)
        a0, a1 = acc_m[h0], acc_m[h1]
        mla = jnp.where(lo_half, a0 / pltpu.roll(a0, SB_DIM, axis=1), a1 / pltpu.roll(a1, SB_DIM, axis=1))
        o_ref[:, SB_W + PAIR_W * p:SB_W + PAIR_W * (p + 1)] = mla.astype(o_ref.dtype)


def _diag_masks(tq, tk, q_pos0, k_pos0):
    qpos = q_pos0 + lax.broadcasted_iota(jnp.int32, (tq, tk), 0)
    kpos = k_pos0 + lax.broadcasted_iota(jnp.int32, (tq, tk), 1)
    shift = CHUNK.bit_length() - 1
    return kpos < qpos, (kpos >> shift) <= (qpos >> shift)


def _attn_prompt_kernel(qsb_ref, qn_ref, qpe_ref, ksb_ref, vsb_ref, kn_ref, kpe_ref, vm_ref, tri_ref,
                        o_ref, acc_sb, acc_m, c_s, m_s):
    t = qsb_ref.shape[1]
    qi = pl.program_id(1)
    q_ops = [_head_operands(qsb_ref.at[0], qn_ref.at[0], qpe_ref.at[0], h) for h in range(SB_HEADS)]
    tri = tri_ref[...]

    def key_block(start):
        rows = pl.ds(start, t)
        return (ksb_ref[0, rows, :], vsb_ref[0, rows, :], kn_ref[0, rows, :], kpe_ref[0, rows, :],
                vm_ref[0, rows, :])

    sb_mask, mla_mask = _diag_masks(t, t, 0, 0)
    k_sb, v_sb, k_n, k_pe, v_m = key_block(pl.multiple_of(qi * t, t))
    _attend_block(q_ops, k_sb, v_sb, k_n, k_pe, v_m, tri, sb_mask, mla_mask, True,
                  acc_sb, acc_m, c_s, m_s)

    def earlier_block(j, with_sb):
        kb = qi - 1 - j
        k_sb, v_sb, k_n, k_pe, v_m = key_block(pl.multiple_of(kb * t, t))
        _attend_block(q_ops, k_sb, v_sb, k_n, k_pe, v_m, tri, None, None, False,
                      acc_sb, acc_m, c_s, m_s, with_sb=with_sb)

    def sb_alive():
        return (jnp.max(c_s[...]) > SB_DEAD).astype(jnp.int32)

    def sb_cond(state):
        j, alive = state
        return jnp.logical_and(j < qi, alive > 0)

    def sb_body(state):
        j, _ = state
        earlier_block(j, True)
        return j + 1, sb_alive()

    j_done, _ = lax.while_loop(sb_cond, sb_body, (jnp.int32(0), sb_alive()))

    def mla_body(j, _):
        earlier_block(j, False)
        return 0

    lax.fori_loop(j_done, qi, mla_body, 0)
    _write_out(o_ref.at[0], acc_sb, acc_m)


def _attn_prompt(qsb, qn, qpe, ksb, vsb, kn, kpe, vm, tri):
    b, s, _ = qsb.shape
    t = tri.shape[0]
    qblk = lambda width: pl.BlockSpec((1, t, width), lambda i, j: (i, j, 0))
    full = lambda width: pl.BlockSpec((1, s, width), lambda i, j: (i, 0, 0))
    return pl.pallas_call(
        _attn_prompt_kernel,
        out_shape=jax.ShapeDtypeStruct((b, s, SB_W + MLA_W), BF16),
        grid=(b, s // t),
        in_specs=[qblk(SB_W), qblk(MLA_W), qblk(PE_W), full(SB_W), full(SB_W), full(MLA_W), full(LANES),
                  full(MLA_W), pl.BlockSpec((t, t), lambda i, j: (0, 0))],
        out_specs=qblk(SB_W + MLA_W),
        scratch_shapes=[pltpu.VMEM((SB_HEADS, t, LANES), F32)] * 4,
        name="attn_prompt",
        compiler_params=pltpu.CompilerParams(dimension_semantics=("arbitrary", "arbitrary"),
                                             vmem_limit_bytes=VMEM_LIMIT),
    )(qsb, qn, qpe, ksb, vsb, kn, kpe, vm, tri)


def _tn_dot(a, b):
    return lax.dot_general(a, b, (((0,), (0,)), ((), ())), preferred_element_type=F32)


def _sample_block(k_sb, v_sb, k_cat, v_m, tri_u, sb_mask, mla_mask, first,
                  qbd_sb, qbd_m, acc_sb, acc_m, c_s, m_s, l_s):
    z = _dot(k_sb, qbd_sb[...])
    log_beta, log_1m = _stick_logs(z, sb_mask)
    suffix = _dot(tri_u, log_1m.astype(BF16))
    col_sum = jnp.sum(log_1m, axis=0, keepdims=True)
    if first:
        wgt = jnp.exp(log_beta + suffix)
        c_s[...] = col_sum
    else:
        carry = c_s[...]
        wgt = jnp.exp(log_beta + suffix + carry)
        c_s[...] = carry + col_sum
    if sb_mask is not None:
        wgt = jnp.where(sb_mask, wgt, 0.0)
    pv = _tn_dot(v_sb, wgt.astype(BF16))
    if first:
        acc_sb[...] = pv
    else:
        acc_sb[...] += pv

    z = _dot(k_cat, qbd_m[...])
    if mla_mask is not None:
        z = jnp.where(mla_mask, z, NEG_BIG)
    zmax = jnp.max(z, axis=0, keepdims=True)
    if first:
        prob = jnp.exp(z - zmax)
        m_s[...] = zmax
        l_s[...] = jnp.sum(prob, axis=0, keepdims=True)
        acc_m[...] = _tn_dot(v_m, prob.astype(BF16))
    else:
        m_old = m_s[...]
        m_new = jnp.maximum(m_old, zmax)
        prob = jnp.exp(z - m_new)
        alpha = jnp.exp(m_old - m_new)
        m_s[...] = m_new
        l_s[...] = alpha * l_s[...] + jnp.sum(prob, axis=0, keepdims=True)
        acc_m[...] = alpha * acc_m[...] + _tn_dot(v_m, prob.astype(BF16))


def _attn_sample_kernel(qsb_ref, qn_ref, qpe_ref, ksb_ref, vsb_ref, kn_ref, kpe_ref, vm_ref,
                        cksb_ref, cvsb_ref, cckv_ref, ckpe_ref, wkk_ref, wkv_ref, gkn_ref, m64_ref, rep_ref,
                        tri_ref, sel_ref,
                        o_ref, qbd_sb, qbd_m, acc_sb, acc_m, c_s, m_s, l_s, *, past_len):
    tq = qsb_ref.shape[1]
    ncol = SB_HEADS * tq
    q_shift = tq.bit_length() - 1
    j = pl.program_id(1)
    state = (qbd_sb, qbd_m, acc_sb, acc_m, c_s, m_s, l_s)

    @pl.when(j == 0)
    def _():
        sel = sel_ref[...]
        row_head = lax.broadcasted_iota(jnp.int32, (SB_W, ncol), 0) >> (SB_DIM.bit_length() - 1)
        col_head = lax.broadcasted_iota(jnp.int32, (SB_W, ncol), 1) >> q_shift
        on_diag = row_head == col_head
        qbd_sb[...] = jnp.where(on_diag, _tn_dot(qsb_ref[0], sel), 0.0).astype(BF16)
        qbd_m[0:MLA_W, :] = jnp.where(on_diag, _tn_dot(qn_ref[0], sel), 0.0).astype(BF16)
        pe_t = _tn_dot(qpe_ref[0], sel)
        head_of_col = lax.broadcasted_iota(jnp.int32, (MLA_ROPE, ncol), 1) >> q_shift
        pe = jnp.zeros((MLA_ROPE, ncol), F32)
        for h in range(MLA_HEADS):
            pe = pe + jnp.where(head_of_col == h, pe_t[MLA_ROPE * h:MLA_ROPE * (h + 1), :], 0.0)
        qbd_m[MLA_W:MLA_W + LANES, :] = jnp.concatenate(
            [pe, jnp.zeros((LANES - MLA_ROPE, ncol), F32)], axis=0).astype(BF16)

        key_i = lax.broadcasted_iota(jnp.int32, (tq, ncol), 0)
        qry_i = lax.broadcasted_iota(jnp.int32, (tq, ncol), 1) & (tq - 1)
        c_shift = CHUNK.bit_length() - 1
        sb_mask = key_i < qry_i
        mla_mask = ((past_len + key_i) >> c_shift) <= ((past_len + qry_i) >> c_shift)
        k_cat = jnp.concatenate([kn_ref[0], kpe_ref[0]], axis=-1)
        _sample_block(ksb_ref[0], vsb_ref[0], k_cat, vm_ref[0], tri_ref[0:tq, 0:tq], sb_mask, mla_mask, True,
                      *state)

    ckv16 = cckv_ref[0, 0].astype(BF16)
    c_kn = _seg_rms(_dot(ckv16, wkk_ref[...]), m64_ref[...], gkn_ref[...]).astype(BF16)
    c_vm = _dot(ckv16, wkv_ref[...]).astype(BF16)
    c_kpe = _dot(ckpe_ref[0, 0].astype(BF16), rep_ref[...]).astype(BF16)
    k_cat = jnp.concatenate([c_kn, c_kpe], axis=-1)
    _sample_block(cksb_ref[0, 0].astype(BF16), cvsb_ref[0, 0].astype(BF16), k_cat, c_vm, tri_ref[...],
                  None, None, False, *state)

    @pl.when(j == pl.num_programs(1) - 1)
    def _():
        lane_head = lax.broadcasted_iota(jnp.int32, (tq, SB_W), 1) >> (SB_DIM.bit_length() - 1)

        def diag_blocks(acc_t):
            acc = acc_t.T
            out = jnp.zeros((tq, SB_W), F32)
            for h in range(SB_HEADS):
                out = out + jnp.where(lane_head == h, acc[tq * h:tq * (h + 1), :], 0.0)
            return out

        o_ref[0, :, 0:SB_W] = diag_blocks(acc_sb[...]).astype(o_ref.dtype)
        o_ref[0, :, SB_W:SB_W + MLA_W] = diag_blocks(acc_m[...] / l_s[...]).astype(o_ref.dtype)


def _attn_sample(layer, qsb, qn, qpe, ksb, vsb, kn, kpe, vm, cache_k, cache_v, cache_ckv, cache_kpe, w, tri_u, sel):
    b, tq, _ = qsb.shape
    weights = [w["w_kk"], w["w_kv"], w["g_mk_nope"], w["m64"], w["rep"]]
    past_len = cache_k.shape[2]
    tk = tri_u.shape[0]
    nkb = past_len // tk
    ncol = SB_HEADS * tq
    new = lambda width: pl.BlockSpec((1, tq, width), lambda i, j: (i, 0, 0))
    rev_l = lambda width: pl.BlockSpec((1, 1, tk, width), lambda i, j: (layer, i, nkb - 1 - j, 0))
    const = lambda a: pl.BlockSpec(a.shape, lambda i, j: (0,) * a.ndim)
    return pl.pallas_call(
        functools.partial(_attn_sample_kernel, past_len=past_len),
        out_shape=jax.ShapeDtypeStruct((b, tq, SB_W + MLA_W), BF16),
        grid=(b, nkb),
        in_specs=[new(SB_W), new(MLA_W), new(PE_W), new(SB_W), new(SB_W), new(MLA_W), new(LANES), new(MLA_W),
                  rev_l(SB_W), rev_l(SB_W), rev_l(KV_RANK), rev_l(MLA_ROPE)] + [const(a) for a in weights] +
                 [const(tri_u), const(sel)],
        out_specs=new(SB_W + MLA_W),
        scratch_shapes=[pltpu.VMEM((SB_W, ncol), BF16), pltpu.VMEM((MLA_W + LANES, ncol), BF16),
                        pltpu.VMEM((SB_W, ncol), F32), pltpu.VMEM((MLA_W, ncol), F32),
                        pltpu.VMEM((1, ncol), F32), pltpu.VMEM((1, ncol), F32), pltpu.VMEM((1, ncol), F32)],
        name="attn_sample",
        compiler_params=pltpu.CompilerParams(dimension_semantics=("arbitrary", "arbitrary"),
                                             vmem_limit_bytes=VMEM_LIMIT),
    )(qsb, qn, qpe, ksb, vsb, kn, kpe, vm, cache_k, cache_v, cache_ckv, cache_kpe, *weights, tri_u, sel)


def _post_kernel(x_ref, o_ref, mod_ref, wout_ref, gffn_ref, wr_ref, br_ref, ltri_ref, wg_ref, wu_ref, wd_ref,
                 out_ref, hs_s, gs_s, perm_s, y_s, meta_s):
    nb, ts, d = x_ref.shape
    r = nb * ts
    e = pl.program_id(2)
    group_shift = EXP_PER_GROUP.bit_length() - 1

    @pl.when(e == 0)
    def _():
        gt_a = mod_ref[:, 2:3, :]
        sh_m = mod_ref[:, 3:4, :]
        sc_m = mod_ref[:, 4:5, :]
        att = _dot(o_ref[...].reshape(r, o_ref.shape[-1]), wout_ref[...])
        x1 = x_ref[...] + gt_a * att.reshape(nb, ts, d)
        out_ref[...] = x1
        h2 = (_rms(x1, gffn_ref[...]) * (1.0 + sc_m) + sh_m).reshape(r, d).astype(BF16)
        logits = _dot(h2, wr_ref[...]) + br_ref[...]
        lane_i = lax.broadcasted_iota(jnp.int32, (r, LANES), 1)
        lane = lane_i.astype(F32)
        group_of_lane = (lane_i >> group_shift).astype(F32)
        neg_inf = jnp.float32(-jnp.inf)
        no_lane = jnp.float32(LANES)
        gl = jnp.where(lane_i < N_GROUPS, logits[:, LANES:], neg_inf)
        gmax = jnp.max(gl, axis=-1, keepdims=True)
        gsel = jnp.min(jnp.where(gl == gmax, lane, no_lane), axis=-1, keepdims=True)
        p_g = 1.0 / jnp.sum(jnp.exp(gl - gmax), axis=-1, keepdims=True)
        in_group = group_of_lane == gsel
        em = jnp.where(in_group, logits[:, :LANES], neg_inf)
        v1 = jnp.max(em, axis=-1, keepdims=True)
        i1 = jnp.min(jnp.where(em == v1, lane, no_lane), axis=-1, keepdims=True)
        em2 = jnp.where(lane == i1, neg_inf, em)
        v2 = jnp.max(em2, axis=-1, keepdims=True)
        i2 = jnp.min(jnp.where(em2 == v2, lane, no_lane), axis=-1, keepdims=True)
        t = jnp.exp(v2 - v1)
        w1 = p_g / (1.0 + t)
        first_of_group = gsel * float(EXP_PER_GROUP)
        gates = jnp.where(lane == i1 - first_of_group, w1, 0.0) + jnp.where(lane == i2 - first_of_group, w1 * t, 0.0)

        in_g = jnp.where(lane == gsel, 1.0, 0.0)
        count = jnp.sum(in_g, axis=0, keepdims=True)
        earlier = _dot(ltri_ref[...], in_g.astype(BF16))
        lane_row = lax.broadcasted_iota(jnp.int32, (1, LANES), 1)
        start = jnp.int32(0)
        start_of = jnp.zeros((1, LANES), F32)
        for g in range(N_GROUPS):
            n_g = jnp.sum(jnp.where(lane_row == g, count, 0.0)).astype(jnp.int32)
            n_tiles = (n_g + (GROUP_ALIGN - 1)) // GROUP_ALIGN
            meta_s[g] = start
            meta_s[N_GROUPS + g] = n_tiles
            start_of = jnp.where(lane_row == g, start.astype(F32), start_of)
            start = start + n_tiles * GROUP_ALIGN
        pos = jnp.sum(in_g * (start_of + earlier), axis=-1, keepdims=True)
        pos_row = jnp.broadcast_to(pos, (r, LANES)).T[0:1, :]
        g_hi = gates.astype(BF16)
        g_lo = (gates - g_hi.astype(F32)).astype(BF16)
        payload = jnp.concatenate([h2, g_hi, g_lo], axis=-1)
        n_rows = perm_s.shape[0]
        chunk = 256
        for c in range(n_rows // chunk):
            rows = slice(c * chunk, (c + 1) * chunk)
            row = (c * chunk + lax.broadcasted_iota(jnp.int32, (chunk, r), 0)).astype(F32)
            perm = jnp.where(row == pos_row, 1.0, 0.0).astype(BF16)
            perm_s[rows, :] = perm
            moved = _dot(perm, payload)
            hs_s[rows, :] = moved[:, :d].astype(BF16)
            gs_s[rows, :] = moved[:, d:d + LANES] + moved[:, d + LANES:]
        y_s[...] = jnp.zeros(y_s.shape, F32)

    first_expert = e * EXPERTS_PER_STEP
    g = first_expert >> group_shift
    j0 = first_expert & (EXP_PER_GROUP - 1)
    bucket_start = meta_s[g]
    lane = lax.broadcasted_iota(jnp.int32, (GROUP_ALIGN, LANES), 1)

    def row_tile(i, _):
        rows = pl.ds(pl.multiple_of(bucket_start + i * GROUP_ALIGN, GROUP_ALIGN), GROUP_ALIGN)
        xs = hs_s[rows, :]
        gs = gs_s[rows, :]
        y = None
        for k in range(EXPERTS_PER_STEP):
            a = _dot(xs, wg_ref[k])
            bb = _dot(xs, wu_ref[k])
            g_e = jnp.sum(jnp.where(lane == j0 + k, gs, 0.0), axis=-1, keepdims=True)
            act = (a / (1.0 + jnp.exp(-a))) * bb * g_e
            contrib = _dot(act.astype(BF16), wd_ref[k])
            y = contrib if y is None else y + contrib
        y_s[rows, :] += y
        return 0

    lax.fori_loop(0, meta_s[N_GROUPS + g], row_tile, 0)

    @pl.when(e == pl.num_programs(2) - 1)
    def _():
        gt_m = mod_ref[:, 5:6, :]
        y = _tn_dot(perm_s[...], y_s[...].astype(BF16))
        out_ref[...] += gt_m * y.reshape(nb, ts, d)


def _post(x, o, mod, w, nb, ts):
    b, s, d = x.shape
    r = nb * ts
    n_rows = r + N_GROUPS * GROUP_ALIGN
    n_rows = -(-n_rows // 256) * 256
    ltri = (jnp.arange(r)[:, None] > jnp.arange(r)[None, :]).astype(BF16)
    tok = lambda width: pl.BlockSpec((nb, ts, width), lambda i, j, e: (i, j, 0))
    const = lambda a: pl.BlockSpec(a.shape, lambda i, j, e: (0,) * a.ndim, pipeline_mode=pl.Buffered(1))
    per_e = lambda a: pl.BlockSpec((EXPERTS_PER_STEP,) + a.shape[1:], lambda i, j, e: (e, 0, 0))
    return pl.pallas_call(
        _post_kernel,
        out_shape=jax.ShapeDtypeStruct((b, s, d), F32),
        grid=(b // nb, s // ts, N_EXPERTS // EXPERTS_PER_STEP),
        in_specs=[pl.BlockSpec((nb, ts, d), lambda i, j, e: (i, j, 0), pipeline_mode=pl.Buffered(1)),
                  pl.BlockSpec((nb, ts, o.shape[-1]), lambda i, j, e: (i, j, 0), pipeline_mode=pl.Buffered(1)),
                  pl.BlockSpec((nb, 6, d), lambda i, j, e: (i, 0, 0)),
                  const(w["w_out"]), const(w["g_ffn"]), const(w["w_r"]), const(w["b_r"]), const(ltri),
                  per_e(w["w_gate"]), per_e(w["w_up"]), per_e(w["w_down"])],
        out_specs=pl.BlockSpec((nb, ts, d), lambda i, j, e: (i, j, 0), pipeline_mode=pl.Buffered(1)),
        scratch_shapes=[pltpu.VMEM((n_rows, d), BF16), pltpu.VMEM((n_rows, LANES), F32),
                        pltpu.VMEM((n_rows, r), BF16), pltpu.VMEM((n_rows, d), F32),
                        pltpu.SMEM((2 * N_GROUPS,), jnp.int32)],
        name="post",
        compiler_params=pltpu.CompilerParams(dimension_semantics=("arbitrary", "arbitrary", "arbitrary"),
                                             vmem_limit_bytes=VMEM_LIMIT),
    )(x, o, mod, w["w_out"], w["g_ffn"], w["w_r"], w["b_r"], ltri, w["w_gate"], w["w_up"], w["w_down"])


def _block_mean(width, seg):
    idx = jnp.arange(width) // seg
    return jnp.where(idx[:, None] == idx[None, :], 1.0 / seg, 0.0).astype(BF16)


def _layer_weights(l, g_mix, w_in, g_sb_q, g_sb_k, g_qa, w_qb, g_kva, w_kvb, g_mq_nope, g_mk_nope, g_mq_pe,
                   g_mk_pe, w_out, g_ffn, w_rg, b_rg, w_re, b_re, w_gate, w_up, w_down):
    d = w_in.shape[1]
    row = lambda g, reps: jnp.tile(g[l], reps).reshape(1, -1)
    kpe_cols = jnp.tile(w_in[l][:, C_KPE:C_KPE + MLA_ROPE], (1, KPE_TILE))
    w_qb3 = w_qb[l].reshape(Q_RANK, MLA_HEADS, MLA_NOPE + MLA_ROPE)
    w_kvb3 = w_kvb[l].reshape(KV_RANK, MLA_HEADS, MLA_NOPE + MLA_V)
    pad = jnp.zeros((d, LANES), F32)
    w_r = jnp.concatenate([pad.at[:, :N_EXPERTS].set(w_re[l]), pad.at[:, :N_GROUPS].set(w_rg[l])], axis=1)
    zrow = jnp.zeros((LANES,), F32)
    b_r = jnp.concatenate([zrow.at[:N_EXPERTS].set(b_re[l]), zrow.at[:N_GROUPS].set(b_rg[l])]).reshape(1, -1)
    rep = (jnp.arange(MLA_ROPE)[:, None] == (jnp.arange(LANES)[None, :] % MLA_ROPE)).astype(BF16)
    return dict(
        g_mix=row(g_mix, 1),
        w_in=jnp.concatenate([w_in[l][:, :C_KPE], kpe_cols], axis=1).astype(BF16),
        g_sb_q=row(g_sb_q, SB_HEADS), g_sb_k=row(g_sb_k, SB_HEADS), g_qa=row(g_qa, 1),
        w_qn=w_qb3[:, :, :MLA_NOPE].reshape(Q_RANK, MLA_W).astype(BF16),
        w_qp=w_qb3[:, :, MLA_NOPE:].reshape(Q_RANK, PE_W).astype(BF16),
        g_kva=row(g_kva, 1),
        w_kk=w_kvb3[:, :, :MLA_NOPE].reshape(KV_RANK, MLA_W).astype(BF16),
        w_kv=w_kvb3[:, :, MLA_NOPE:].reshape(KV_RANK, MLA_W).astype(BF16),
        g_mq_nope=row(g_mq_nope, MLA_HEADS), g_mk_nope=row(g_mk_nope, MLA_HEADS),
        g_mq_pe=row(g_mq_pe, MLA_HEADS), g_mk_pe=row(g_mk_pe, KPE_TILE),
        m64=_block_mean(SB_W, SB_DIM), m32=_block_mean(PE_W, MLA_ROPE), rep=rep,
        w_out=w_out[l].astype(BF16), g_ffn=row(g_ffn, 1), w_r=w_r.astype(BF16), b_r=b_r,
        w_gate=w_gate[l].astype(BF16), w_up=w_up[l].astype(BF16), w_down=w_down[l].astype(BF16),
    )


def _rope_tables(pos):
    inv = ROPE_THETA ** (-jnp.arange(0, MLA_ROPE, 2, dtype=F32) / MLA_ROPE)
    ang = pos.astype(F32)[:, None] * inv[None, :]
    cos, sin = jnp.cos(ang), jnp.sin(ang)
    cos = jnp.tile(jnp.concatenate([cos, cos], axis=-1), (1, MLA_HEADS))
    sin = jnp.tile(jnp.concatenate([-sin, sin], axis=-1), (1, MLA_HEADS))
    return cos[None], sin[None]


def _largest_tile(n, cap):
    t = min(n, cap)
    while n % t:
        t -= 8
    return t


def kernel(x_prompt, x_sample, cache_sb_k, cache_sb_v, cache_mla_ckv, cache_mla_kpe, c_prompt, c_sample, w_ada, b_ada, g_mix, w_in, g_sb_q, g_sb_k, g_qa, w_qb, g_kva, w_kvb, g_mq_nope, g_mk_nope, g_mq_pe, g_mk_pe, w_out, g_ffn, w_rg, b_rg, w_re, b_re, w_gate, w_up, w_down):
    bp, sp, d = x_prompt.shape
    bs, ss, _ = x_sample.shape
    depth = w_ada.shape[0]
    past_len = cache_sb_k.shape[2]
    t_attn = 256
    t_cache = 512
    assert sp % t_attn == 0 and past_len % t_cache == 0 and ss % 8 == 0 and ss & (ss - 1) == 0

    mod = _ada(jnp.concatenate([c_prompt, c_sample], axis=0), w_ada, b_ada).reshape(depth, bp + bs, 6, d)
    cos_p, sin_p = _rope_tables(jnp.arange(sp, dtype=jnp.int32))
    cos_s, sin_s = _rope_tables(past_len + jnp.arange(ss, dtype=jnp.int32))
    tri = (jnp.arange(t_attn)[:, None] > jnp.arange(t_attn)[None, :]).astype(BF16)
    tri_u = (jnp.arange(t_cache)[:, None] < jnp.arange(t_cache)[None, :]).astype(BF16)
    sel = (jnp.arange(ss)[:, None] == (jnp.arange(SB_HEADS * ss)[None, :] % ss)).astype(BF16)
    cache_k = cache_sb_k.reshape(depth, bs, past_len, SB_W)
    cache_v = cache_sb_v.reshape(depth, bs, past_len, SB_W)

    ts_p = _largest_tile(sp, 512)
    ts_post = _largest_tile(sp, 1024)
    yp, ys = x_prompt, x_sample
    st_p, st_s = None, None
    for l in range(depth):
        w = _layer_weights(l, g_mix, w_in, g_sb_q, g_sb_k, g_qa, w_qb, g_kva, w_kvb, g_mq_nope, g_mk_nope,
                           g_mq_pe, g_mk_pe, w_out, g_ffn, w_rg, b_rg, w_re, b_re, w_gate, w_up, w_down)
        mod_p, mod_s = mod[l, :bp], mod[l, bp:]

        st_p, (qsb, ksb, vsb, qn, qpe, kn, vm, kpe) = _pre(yp, mod_p, cos_p, sin_p, w, 1, ts_p, l, depth, st_p)
        o = _attn_prompt(qsb, qn, qpe, ksb, vsb, kn, kpe, vm, tri)
        yp = _post(yp, o, mod_p, w, 1, ts_post)

        st_s, (qsb, ksb, vsb, qn, qpe, kn, vm, kpe) = _pre(ys, mod_s, cos_s, sin_s, w, bs, ss, l, depth, st_s)
        o = _attn_sample(l, qsb, qn, qpe, ksb, vsb, kn, kpe, vm, cache_k, cache_v, cache_mla_ckv, cache_mla_kpe,
                         w, tri_u, sel)
        ys = _post(ys, o, mod_s, w, bs, ss)

    heads = lambda a: a.reshape(*a.shape[:3], SB_HEADS, SB_DIM)
    return (yp, ys, heads(st_p[0]), heads(st_p[1]), st_p[2], st_p[3],
            heads(st_s[0]), heads(st_s[1]), st_s[2], st_s[3])
```

```python
import functools
import math

import jax
import jax.numpy as jnp
from jax import lax
from jax.experimental import pallas as pl
from jax.experimental.pallas import tpu as pltpu

F32 = jnp.float32
BF16 = jnp.bfloat16

EPS = 1e-6
CHUNK = 64
SB_HEADS = 8
SB_DIM = 64
SB_W = SB_HEADS * SB_DIM
MLA_HEADS = 8
MLA_NOPE = 64
MLA_ROPE = 32
MLA_V = 64
Q_RANK = 384
KV_RANK = 256
ROPE_THETA = 10000.0
MLA_W = MLA_HEADS * MLA_V
N_GROUPS = 4
EXP_PER_GROUP = 8
N_EXPERTS = N_GROUPS * EXP_PER_GROUP
D_EXPERT = 256

LANES = 128
PAIR_W = 2 * SB_DIM
N_PAIRS = SB_HEADS // 2
KPE_TILE = LANES // MLA_ROPE
PE_W = MLA_HEADS * MLA_ROPE
C_QSB, C_KSB, C_VSB = 0, SB_W, 2 * SB_W
C_QLAT = 3 * SB_W
C_KVLAT = C_QLAT + Q_RANK
C_KPE = C_KVLAT + KV_RANK
IN_COLS_PAD = C_KPE + LANES
MLA_SCALE = 1.0 / math.sqrt(MLA_NOPE + MLA_ROPE)
NEG_BIG = -1e30
EXPERTS_PER_STEP = 4
GROUP_ALIGN = 256
SB_DEAD = -104.0
VMEM_LIMIT = 56 * 1024 * 1024


def _nt_dot(a, b):
    return lax.dot_general(a, b, (((1,), (1,)), ((), ())), preferred_element_type=F32)


def _dot(a, b):
    return jnp.dot(a, b, preferred_element_type=F32)


def _rms(x, g):
    ms = jnp.mean(x * x, axis=-1, keepdims=True)
    return x * lax.rsqrt(ms + EPS) * g


def _seg_rms(x, seg_mean, g):
    ms = _dot((x * x).astype(BF16), seg_mean)
    return x * lax.rsqrt(ms + EPS) * g


def _rot_half(x, lane_in_group_lt_half):
    w = x.shape[-1]
    fwd = pltpu.roll(x, w - MLA_ROPE // 2, axis=1)
    bwd = pltpu.roll(x, MLA_ROPE // 2, axis=1)
    return jnp.where(lane_in_group_lt_half, fwd, bwd)


def _ada_kernel(c_ref, w_ref, b_ref, o_ref):
    c = c_ref[...]
    s = c / (1.0 + jnp.exp(-c))
    o_ref[0] = _dot(s.astype(BF16), w_ref[0].astype(BF16)) + b_ref[0]


def _ada(c_all, w_ada, b_ada):
    depth, d, n = w_ada.shape
    nb = c_all.shape[0]
    tn = d
    return pl.pallas_call(
        _ada_kernel,
        out_shape=jax.ShapeDtypeStruct((depth, nb, n), F32),
        grid=(depth, n // tn),
        in_specs=[
            pl.BlockSpec((nb, d), lambda l, j: (0, 0)),
            pl.BlockSpec((1, d, tn), lambda l, j: (l, 0, j)),
            pl.BlockSpec((1, 1, tn), lambda l, j: (l, 0, j)),
        ],
        out_specs=pl.BlockSpec((1, nb, tn), lambda l, j: (l, 0, j)),
        name="ada",
        compiler_params=pltpu.CompilerParams(dimension_semantics=("arbitrary", "arbitrary")),
    )(c_all, w_ada, b_ada.reshape(depth, 1, n))


def _pre_kernel(x_ref, mod_ref, cos_ref, sin_ref, gmix_ref, win_ref, gsq_ref, gsk_ref, gqa_ref,
                wqn_ref, wqp_ref, gkva_ref, wkk_ref, wkv_ref, gqn_ref, gkn_ref, gqpe_ref, gkpe_ref,
                m64_ref, m32_ref,
                ksb_o, vsb_o, ckv_o, kpe_o,
                qsb_b, ksb_b, vsb_b, qn_b, qpe_b, kn_b, vm_b, kpe_b):
    nb, ts, d = x_ref.shape
    r = nb * ts
    x = x_ref[...]
    sh = mod_ref[:, 0:1, :]
    sc = mod_ref[:, 1:2, :]
    h = _rms(x, gmix_ref[...]) * (1.0 + sc) + sh
    u = _dot(h.reshape(r, d).astype(BF16), win_ref[...])

    m64 = m64_ref[...]
    q_sb = _seg_rms(u[:, C_QSB:C_QSB + SB_W], m64, gsq_ref[...])
    k_sb = _seg_rms(u[:, C_KSB:C_KSB + SB_W], m64, gsk_ref[...])
    v_sb = u[:, C_VSB:C_VSB + SB_W]
    ksb_o[0] = k_sb.reshape(nb, ts, SB_W)
    vsb_o[0] = v_sb.reshape(nb, ts, SB_W)
    qsb_b[...] = (q_sb * (1.0 / math.sqrt(SB_DIM))).astype(BF16).reshape(nb, ts, SB_W)
    ksb_b[...] = k_sb.astype(BF16).reshape(nb, ts, SB_W)
    vsb_b[...] = v_sb.astype(BF16).reshape(nb, ts, SB_W)

    cos = cos_ref[...]
    sin = sin_ref[...]
    lane = lax.broadcasted_iota(jnp.int32, (1, PE_W), 1)
    first_half = (lane & (MLA_ROPE - 1)) < (MLA_ROPE // 2)

    q_lat = _rms(u[:, C_QLAT:C_QLAT + Q_RANK], gqa_ref[...]).astype(BF16)
    qn = _seg_rms(_dot(q_lat, wqn_ref[...]), m64, gqn_ref[...])
    qn_b[...] = (qn * MLA_SCALE).astype(BF16).reshape(nb, ts, MLA_W)
    qp = _seg_rms(_dot(q_lat, wqp_ref[...]), m32_ref[...], gqpe_ref[...])
    qp_rot = _rot_half(qp, first_half)
    qp = qp.reshape(nb, ts, PE_W) * cos + qp_rot.reshape(nb, ts, PE_W) * sin
    qpe_b[...] = (qp * MLA_SCALE).astype(BF16)

    ckv = _rms(u[:, C_KVLAT:C_KVLAT + KV_RANK], gkva_ref[...])
    ckv_o[0] = ckv.reshape(nb, ts, KV_RANK)
    ckv16 = ckv.astype(BF16)
    kn = _seg_rms(_dot(ckv16, wkk_ref[...]), m64, gkn_ref[...])
    kn_b[...] = kn.astype(BF16).reshape(nb, ts, MLA_W)
    vm_b[...] = _dot(ckv16, wkv_ref[...]).astype(BF16).reshape(nb, ts, MLA_W)

    kp = _rms(u[:, C_KPE:C_KPE + LANES], gkpe_ref[...])
    kp_rot = _rot_half(kp, first_half[:, :LANES])
    kp = kp.reshape(nb, ts, LANES) * cos[:, :, :LANES] + kp_rot.reshape(nb, ts, LANES) * sin[:, :, :LANES]
    kpe_o[0] = kp[:, :, :MLA_ROPE]
    kpe_b[...] = kp.astype(BF16)


def _pre_kernel_aliased(*refs):
    _pre_kernel(*refs[N_STATE:])


N_STATE = 4


def _pre(x, mod, cos, sin, w, nb, ts, layer, depth, state):
    b, s, d = x.shape
    grid = (b // nb, s // ts)
    tok = lambda width: pl.BlockSpec((nb, ts, width), lambda i, j: (i, j, 0))
    slab = lambda width: pl.BlockSpec((1, nb, ts, width), lambda i, j: (layer, i, j, 0))
    const = lambda a: pl.BlockSpec(a.shape, lambda i, j: (0,) * a.ndim)
    weights = [w["g_mix"], w["w_in"], w["g_sb_q"], w["g_sb_k"], w["g_qa"], w["w_qn"], w["w_qp"], w["g_kva"],
               w["w_kk"], w["w_kv"], w["g_mq_nope"], w["g_mk_nope"], w["g_mq_pe"], w["g_mk_pe"],
               w["m64"], w["m32"]]
    state_widths = (SB_W, SB_W, KV_RANK, MLA_ROPE)
    bf16_widths = (SB_W, SB_W, SB_W, MLA_W, PE_W, MLA_W, MLA_W, LANES)
    out_shape = tuple(jax.ShapeDtypeStruct((depth, b, s, wd), F32) for wd in state_widths) + \
        tuple(jax.ShapeDtypeStruct((b, s, wd), BF16) for wd in bf16_widths)
    in_specs = [tok(d),
                pl.BlockSpec((nb, 6, d), lambda i, j: (i, 0, 0)),
                pl.BlockSpec((1, ts, PE_W), lambda i, j: (0, j, 0)),
                pl.BlockSpec((1, ts, PE_W), lambda i, j: (0, j, 0))] + [const(a) for a in weights]
    args = (x, mod, cos, sin, *weights)
    body, aliases = _pre_kernel, {}
    if state is not None:
        body, aliases = _pre_kernel_aliased, {i: i for i in range(N_STATE)}
        in_specs = [pl.BlockSpec(memory_space=pl.ANY)] * N_STATE + in_specs
        args = (*state, *args)
    outs = pl.pallas_call(
        body,
        out_shape=out_shape,
        grid=grid,
        in_specs=in_specs,
        out_specs=tuple(slab(wd) for wd in state_widths) + tuple(tok(wd) for wd in bf16_widths),
        input_output_aliases=aliases,
        name="pre",
        compiler_params=pltpu.CompilerParams(dimension_semantics=("arbitrary", "arbitrary"),
                                             vmem_limit_bytes=VMEM_LIMIT),
    )(*args)
    return outs[:N_STATE], outs[N_STATE:]


def _stick_logs(z, mask):
    sign_bit = jnp.uint32(0x80000000)
    neg_abs = lax.bitcast_convert_type(lax.bitcast_convert_type(z, jnp.uint32) | sign_bit, F32)
    log_beta = jnp.minimum(z, 0.0) - jnp.log(1.0 + jnp.exp(neg_abs))
    log_1m = log_beta - z
    if mask is not None:
        log_1m = jnp.where(mask, log_1m, 0.0)
    return log_beta, log_1m


def _head_operands(qsb_ref, qn_ref, qpe_ref, h):
    p, e = divmod(h, 2)
    lane = lax.broadcasted_iota(jnp.int32, (1, LANES), 1)
    half = (lane >= SB_DIM) if e == 1 else (lane < SB_DIM)
    sl = slice(PAIR_W * p, PAIR_W * (p + 1))
    q_sb = jnp.where(half, qsb_ref[:, sl], jnp.zeros((), BF16))
    q_n = jnp.where(half, qn_ref[:, sl], jnp.zeros((), BF16))
    slab, pos = divmod(h, KPE_TILE)
    pe_lanes = (lane >= MLA_ROPE * pos) & (lane < MLA_ROPE * (pos + 1))
    q_pe = jnp.where(pe_lanes, qpe_ref[:, LANES * slab:LANES * (slab + 1)], jnp.zeros((), BF16))
    return q_sb, jnp.concatenate([q_n, q_pe], axis=-1)


def _attend_block(q_ops, k_sb, v_sb, k_n, k_pe, v_m, tri, sb_mask, mla_mask, first,
                  acc_sb, acc_m, c_s, m_s, with_sb=True):
    heads = range(SB_HEADS)
    slab = lambda a, h: a[:, PAIR_W * (h // 2):PAIR_W * (h // 2 + 1)]
    kcat = [jnp.concatenate([k_n[:, PAIR_W * p:PAIR_W * (p + 1)], k_pe], axis=-1) for p in range(N_PAIRS)]

    z_sb = [_nt_dot(q_ops[h][0], slab(k_sb, h)) for h in heads] if with_sb else []
    z_m = [_nt_dot(q_ops[h][1], kcat[h // 2]) for h in heads]

    log_w, suffix = [], []
    for h in heads if with_sb else ():
        log_beta, log_1m = _stick_logs(z_sb[h], sb_mask)
        suffix.append(_dot(log_1m.astype(BF16), tri))
        row_sum = jnp.sum(log_1m, axis=-1, keepdims=True)
        if first:
            log_w.append(log_beta)
            c_s[h] = jnp.broadcast_to(row_sum, (log_1m.shape[0], LANES))
        else:
            carry = c_s[h]
            log_w.append(log_beta + jnp.concatenate([carry] * (log_beta.shape[1] // LANES), axis=-1))
            c_s[h] = carry + row_sum

    lane = lax.broadcasted_iota(jnp.int32, (1, LANES), 1)
    for h in heads:
        z = z_m[h]
        if mla_mask is not None:
            z = jnp.where(mla_mask, z, NEG_BIG)
        tq = z.shape[0]
        halves = [z[:, LANES * i:LANES * (i + 1)] for i in range(z.shape[1] // LANES)]
        zmax = jnp.max(functools.reduce(jnp.maximum, halves), axis=-1, keepdims=True)
        own_half = (lane >= SB_DIM) if h % 2 else (lane < SB_DIM)
        v_ones = jnp.where(own_half, slab(v_m, h), jnp.ones((), BF16))
        if first:
            m_new = jnp.broadcast_to(zmax, (tq, LANES))
        else:
            m_old = m_s[h]
            m_new = jnp.maximum(m_old, zmax)
        prob = jnp.concatenate([jnp.exp(zh - m_new) for zh in halves], axis=-1).astype(BF16)
        m_s[h] = m_new
        if first:
            acc_m[h] = _dot(prob, v_ones)
        else:
            acc_m[h] = jnp.exp(m_old - m_new) * acc_m[h] + _dot(prob, v_ones)

    for h in heads if with_sb else ():
        wgt = jnp.exp(log_w[h] + suffix[h])
        if sb_mask is not None:
            wgt = jnp.where(sb_mask, wgt, 0.0)
        pv = _dot(wgt.astype(BF16), slab(v_sb, h))
        if first:
            acc_sb[h] = pv
        else:
            acc_sb[h] += pv


def _write_out(o_ref, acc_sb, acc_m):
    lane = lax.broadcasted_iota(jnp.int32, (1, LANES), 1)
    lo_half = lane < SB_DIM
    for p in range(N_PAIRS):
        h0, h1 = 2 * p, 2 * p + 1
        o_ref[:, PAIR_W * p:PAIR_W * (p + 1)] = jnp.where(lo_half, acc_sb[h0], acc_sb[h1]).astype(o_ref.dtype)
        a0, a1 = acc_m[h0], acc_m[h1]
        mla = jnp.where(lo_half, a0 / pltpu.roll(a0, SB_DIM, axis=1), a1 / pltpu.roll(a1, SB_DIM, axis=1))
        o_ref[:, SB_W + PAIR_W * p:SB_W + PAIR_W * (p + 1)] = mla.astype(o_ref.dtype)


def _diag_masks(tq, tk, q_pos0, k_pos0):
    qpos = q_pos0 + lax.broadcasted_iota(jnp.int32, (tq, tk), 0)
    kpos = k_pos0 + lax.broadcasted_iota(jnp.int32, (tq, tk), 1)
    shift = CHUNK.bit_length() - 1
    return kpos < qpos, (kpos >> shift) <= (qpos >> shift)


def _attn_prompt_kernel(qsb_ref, qn_ref, qpe_ref, ksb_ref, vsb_ref, kn_ref, kpe_ref, vm_ref, tri_ref,
                        o_ref, acc_sb, acc_m, c_s, m_s):
    t = qsb_ref.shape[1]
    qi = pl.program_id(1)
    q_ops = [_head_operands(qsb_ref.at[0], qn_ref.at[0], qpe_ref.at[0], h) for h in range(SB_HEADS)]
    tri = tri_ref[...]

    def key_block(start):
        rows = pl.ds(start, t)
        return (ksb_ref[0, rows, :], vsb_ref[0, rows, :], kn_ref[0, rows, :], kpe_ref[0, rows, :],
                vm_ref[0, rows, :])

    sb_mask, mla_mask = _diag_masks(t, t, 0, 0)
    k_sb, v_sb, k_n, k_pe, v_m = key_block(pl.multiple_of(qi * t, t))
    _attend_block(q_ops, k_sb, v_sb, k_n, k_pe, v_m, tri, sb_mask, mla_mask, True,
                  acc_sb, acc_m, c_s, m_s)

    def earlier_block(j, with_sb):
        kb = qi - 1 - j
        k_sb, v_sb, k_n, k_pe, v_m = key_block(pl.multiple_of(kb * t, t))
        _attend_block(q_ops, k_sb, v_sb, k_n, k_pe, v_m, tri, None, None, False,
                      acc_sb, acc_m, c_s, m_s, with_sb=with_sb)

    def sb_alive():
        return (jnp.max(c_s[...]) > SB_DEAD).astype(jnp.int32)

    def sb_cond(state):
        j, alive = state
        return jnp.logical_and(j < qi, alive > 0)

    def sb_body(state):
        j, _ = state
        earlier_block(j, True)
        return j + 1, sb_alive()

    j_done, _ = lax.while_loop(sb_cond, sb_body, (jnp.int32(0), sb_alive()))

    def mla_body(j, _):
        earlier_block(j, False)
        return 0

    lax.fori_loop(j_done, qi, mla_body, 0)
    _write_out(o_ref.at[0], acc_sb, acc_m)


def _attn_prompt(qsb, qn, qpe, ksb, vsb, kn, kpe, vm, tri):
    b, s, _ = qsb.shape
    t = tri.shape[0]
    qblk = lambda width: pl.BlockSpec((1, t, width), lambda i, j: (i, j, 0))
    full = lambda width: pl.BlockSpec((1, s, width), lambda i, j: (i, 0, 0))
    return pl.pallas_call(
        _attn_prompt_kernel,
        out_shape=jax.ShapeDtypeStruct((b, s, SB_W + MLA_W), BF16),
        grid=(b, s // t),
        in_specs=[qblk(SB_W), qblk(MLA_W), qblk(PE_W), full(SB_W), full(SB_W), full(MLA_W), full(LANES),
                  full(MLA_W), pl.BlockSpec((t, t), lambda i, j: (0, 0))],
        out_specs=qblk(SB_W + MLA_W),
        scratch_shapes=[pltpu.VMEM((SB_HEADS, t, LANES), F32)] * 4,
        name="attn_prompt",
        compiler_params=pltpu.CompilerParams(dimension_semantics=("arbitrary", "arbitrary"),
                                             vmem_limit_bytes=VMEM_LIMIT),
    )(qsb, qn, qpe, ksb, vsb, kn, kpe, vm, tri)


def _tn_dot(a, b):
    return lax.dot_general(a, b, (((0,), (0,)), ((), ())), preferred_element_type=F32)


def _sample_block(k_sb, v_sb, k_cat, v_m, tri_u, sb_mask, mla_mask, first,
                  qbd_sb, qbd_m, acc_sb, acc_m, c_s, m_s, l_s):
    z = _dot(k_sb, qbd_sb[...])
    log_beta, log_1m = _stick_logs(z, sb_mask)
    suffix = _dot(tri_u, log_1m.astype(BF16))
    col_sum = jnp.sum(log_1m, axis=0, keepdims=True)
    if first:
        wgt = jnp.exp(log_beta + suffix)
        c_s[...] = col_sum
    else:
        carry = c_s[...]
        wgt = jnp.exp(log_beta + suffix + carry)
        c_s[...] = carry + col_sum
    if sb_mask is not None:
        wgt = jnp.where(sb_mask, wgt, 0.0)
    pv = _tn_dot(v_sb, wgt.astype(BF16))
    if first:
        acc_sb[...] = pv
    else:
        acc_sb[...] += pv

    z = _dot(k_cat, qbd_m[...])
    if mla_mask is not None:
        z = jnp.where(mla_mask, z, NEG_BIG)
    zmax = jnp.max(z, axis=0, keepdims=True)
    if first:
        prob = jnp.exp(z - zmax)
        m_s[...] = zmax
        l_s[...] = jnp.sum(prob, axis=0, keepdims=True)
        acc_m[...] = _tn_dot(v_m, prob.astype(BF16))
    else:
        m_old = m_s[...]
        m_new = jnp.maximum(m_old, zmax)
        prob = jnp.exp(z - m_new)
        alpha = jnp.exp(m_old - m_new)
        m_s[...] = m_new
        l_s[...] = alpha * l_s[...] + jnp.sum(prob, axis=0, keepdims=True)
        acc_m[...] = alpha * acc_m[...] + _tn_dot(v_m, prob.astype(BF16))


def _attn_sample_kernel(qsb_ref, qn_ref, qpe_ref, ksb_ref, vsb_ref, kn_ref, kpe_ref, vm_ref,
                        cksb_ref, cvsb_ref, cckv_ref, ckpe_ref, wkk_ref, wkv_ref, gkn_ref, m64_ref, rep_ref,
                        tri_ref, sel_ref,
                        o_ref, qbd_sb, qbd_m, acc_sb, acc_m, c_s, m_s, l_s, *, past_len):
    tq = qsb_ref.shape[1]
    ncol = SB_HEADS * tq
    q_shift = tq.bit_length() - 1
    j = pl.program_id(1)
    state = (qbd_sb, qbd_m, acc_sb, acc_m, c_s, m_s, l_s)

    @pl.when(j == 0)
    def _():
        sel = sel_ref[...]
        row_head = lax.broadcasted_iota(jnp.int32, (SB_W, ncol), 0) >> (SB_DIM.bit_length() - 1)
        col_head = lax.broadcasted_iota(jnp.int32, (SB_W, ncol), 1) >> q_shift
        on_diag = row_head == col_head
        qbd_sb[...] = jnp.where(on_diag, _tn_dot(qsb_ref[0], sel), 0.0).astype(BF16)
        qbd_m[0:MLA_W, :] = jnp.where(on_diag, _tn_dot(qn_ref[0], sel), 0.0).astype(BF16)
        pe_t = _tn_dot(qpe_ref[0], sel)
        head_of_col = lax.broadcasted_iota(jnp.int32, (MLA_ROPE, ncol), 1) >> q_shift
        pe = jnp.zeros((MLA_ROPE, ncol), F32)
        for h in range(MLA_HEADS):
            pe = pe + jnp.where(head_of_col == h, pe_t[MLA_ROPE * h:MLA_ROPE * (h + 1), :], 0.0)
        qbd_m[MLA_W:MLA_W + LANES, :] = jnp.concatenate(
            [pe, jnp.zeros((LANES - MLA_ROPE, ncol), F32)], axis=0).astype(BF16)

        key_i = lax.broadcasted_iota(jnp.int32, (tq, ncol), 0)
        qry_i = lax.broadcasted_iota(jnp.int32, (tq, ncol), 1) & (tq - 1)
        c_shift = CHUNK.bit_length() - 1
        sb_mask = key_i < qry_i
        mla_mask = ((past_len + key_i) >> c_shift) <= ((past_len + qry_i) >> c_shift)
        k_cat = jnp.concatenate([kn_ref[0], kpe_ref[0]], axis=-1)
        _sample_block(ksb_ref[0], vsb_ref[0], k_cat, vm_ref[0], tri_ref[0:tq, 0:tq], sb_mask, mla_mask, True,
                      *state)

    ckv16 = cckv_ref[0, 0].astype(BF16)
    c_kn = _seg_rms(_dot(ckv16, wkk_ref[...]), m64_ref[...], gkn_ref[...]).astype(BF16)
    c_vm = _dot(ckv16, wkv_ref[...]).astype(BF16)
    c_kpe = _dot(ckpe_ref[0, 0].astype(BF16), rep_ref[...]).astype(BF16)
    k_cat = jnp.concatenate([c_kn, c_kpe], axis=-1)
    _sample_block(cksb_ref[0, 0].astype(BF16), cvsb_ref[0, 0].astype(BF16), k_cat, c_vm, tri_ref[...],
                  None, None, False, *state)

    @pl.when(j == pl.num_programs(1) - 1)
    def _():
        lane_head = lax.broadcasted_iota(jnp.int32, (tq, SB_W), 1) >> (SB_DIM.bit_length() - 1)

        def diag_blocks(acc_t):
            acc = acc_t.T
            out = jnp.zeros((tq, SB_W), F32)
            for h in range(SB_HEADS):
                out = out + jnp.where(lane_head == h, acc[tq * h:tq * (h + 1), :], 0.0)
            return out

        o_ref[0, :, 0:SB_W] = diag_blocks(acc_sb[...]).astype(o_ref.dtype)
        o_ref[0, :, SB_W:SB_W + MLA_W] = diag_blocks(acc_m[...] / l_s[...]).astype(o_ref.dtype)


def _attn_sample(layer, qsb, qn, qpe, ksb, vsb, kn, kpe, vm, cache_k, cache_v, cache_ckv, cache_kpe, w, tri_u, sel):
    b, tq, _ = qsb.shape
    weights = [w["w_kk"], w["w_kv"], w["g_mk_nope"], w["m64"], w["rep"]]
    past_len = cache_k.shape[2]
    tk = tri_u.shape[0]
    nkb = past_len // tk
    ncol = SB_HEADS * tq
    new = lambda width: pl.BlockSpec((1, tq, width), lambda i, j: (i, 0, 0))
    rev_l = lambda width: pl.BlockSpec((1, 1, tk, width), lambda i, j: (layer, i, nkb - 1 - j, 0))
    const = lambda a: pl.BlockSpec(a.shape, lambda i, j: (0,) * a.ndim)
    return pl.pallas_call(
        functools.partial(_attn_sample_kernel, past_len=past_len),
        out_shape=jax.ShapeDtypeStruct((b, tq, SB_W + MLA_W), BF16),
        grid=(b, nkb),
        in_specs=[new(SB_W), new(MLA_W), new(PE_W), new(SB_W), new(SB_W), new(MLA_W), new(LANES), new(MLA_W),
                  rev_l(SB_W), rev_l(SB_W), rev_l(KV_RANK), rev_l(MLA_ROPE)] + [const(a) for a in weights] +
                 [const(tri_u), const(sel)],
        out_specs=new(SB_W + MLA_W),
        scratch_shapes=[pltpu.VMEM((SB_W, ncol), BF16), pltpu.VMEM((MLA_W + LANES, ncol), BF16),
                        pltpu.VMEM((SB_W, ncol), F32), pltpu.VMEM((MLA_W, ncol), F32),
                        pltpu.VMEM((1, ncol), F32), pltpu.VMEM((1, ncol), F32), pltpu.VMEM((1, ncol), F32)],
        name="attn_sample",
        compiler_params=pltpu.CompilerParams(dimension_semantics=("arbitrary", "arbitrary"),
                                             vmem_limit_bytes=VMEM_LIMIT),
    )(qsb, qn, qpe, ksb, vsb, kn, kpe, vm, cache_k, cache_v, cache_ckv, cache_kpe, *weights, tri_u, sel)


def _post_kernel(x_ref, o_ref, mod_ref, wout_ref, gffn_ref, wr_ref, br_ref, ltri_ref, wg_ref, wu_ref, wd_ref,
                 out_ref, hs_s, gs_s, perm_s, y_s, meta_s):
    nb, ts, d = x_ref.shape
    r = nb * ts
    e = pl.program_id(2)
    group_shift = EXP_PER_GROUP.bit_length() - 1

    @pl.when(e == 0)
    def _():
        gt_a = mod_ref[:, 2:3, :]
        sh_m = mod_ref[:, 3:4, :]
        sc_m = mod_ref[:, 4:5, :]
        att = _dot(o_ref[...].reshape(r, o_ref.shape[-1]), wout_ref[...])
        x1 = x_ref[...] + gt_a * att.reshape(nb, ts, d)
        out_ref[...] = x1
        h2 = (_rms(x1, gffn_ref[...]) * (1.0 + sc_m) + sh_m).reshape(r, d).astype(BF16)
        logits = _dot(h2, wr_ref[...]) + br_ref[...]
        lane_i = lax.broadcasted_iota(jnp.int32, (r, LANES), 1)
        lane = lane_i.astype(F32)
        group_of_lane = (lane_i >> group_shift).astype(F32)
        neg_inf = jnp.float32(-jnp.inf)
        no_lane = jnp.float32(LANES)
        gl = jnp.where(lane_i < N_GROUPS, logits[:, LANES:], neg_inf)
        gmax = jnp.max(gl, axis=-1, keepdims=True)
        gsel = jnp.min(jnp.where(gl == gmax, lane, no_lane), axis=-1, keepdims=True)
        p_g = 1.0 / jnp.sum(jnp.exp(gl - gmax), axis=-1, keepdims=True)
        in_group = group_of_lane == gsel
        em = jnp.where(in_group, logits[:, :LANES], neg_inf)
        v1 = jnp.max(em, axis=-1, keepdims=True)
        i1 = jnp.min(jnp.where(em == v1, lane, no_lane), axis=-1, keepdims=True)
        em2 = jnp.where(lane == i1, neg_inf, em)
        v2 = jnp.max(em2, axis=-1, keepdims=True)
        i2 = jnp.min(jnp.where(em2 == v2, lane, no_lane), axis=-1, keepdims=True)
        t = jnp.exp(v2 - v1)
        w1 = p_g / (1.0 + t)
        first_of_group = gsel * float(EXP_PER_GROUP)
        gates = jnp.where(lane == i1 - first_of_group, w1, 0.0) + jnp.where(lane == i2 - first_of_group, w1 * t, 0.0)

        in_g = jnp.where(lane == gsel, 1.0, 0.0)
        count = jnp.sum(in_g, axis=0, keepdims=True)
        earlier = _dot(ltri_ref[...], in_g.astype(BF16))
        lane_row = lax.broadcasted_iota(jnp.int32, (1, LANES), 1)
        start = jnp.int32(0)
        start_of = jnp.zeros((1, LANES), F32)
        for g in range(N_GROUPS):
            n_g = jnp.sum(jnp.where(lane_row == g, count, 0.0)).astype(jnp.int32)
            n_tiles = (n_g + (GROUP_ALIGN - 1)) // GROUP_ALIGN
            meta_s[g] = start
            meta_s[N_GROUPS + g] = n_tiles
            start_of = jnp.where(lane_row == g, start.astype(F32), start_of)
            start = start + n_tiles * GROUP_ALIGN
        pos = jnp.sum(in_g * (start_of + earlier), axis=-1, keepdims=True)
        pos_row = jnp.broadcast_to(pos, (r, LANES)).T[0:1, :]
        g_hi = gates.astype(BF16)
        g_lo = (gates - g_hi.astype(F32)).astype(BF16)
        payload = jnp.concatenate([h2, g_hi, g_lo], axis=-1)
        n_rows = perm_s.shape[0]
        chunk = 256
        for c in range(n_rows // chunk):
            rows = slice(c * chunk, (c + 1) * chunk)
            row = (c * chunk + lax.broadcasted_iota(jnp.int32, (chunk, r), 0)).astype(F32)
            perm = jnp.where(row == pos_row, 1.0, 0.0).astype(BF16)
            perm_s[rows, :] = perm
            moved = _dot(perm, payload)
            hs_s[rows, :] = moved[:, :d].astype(BF16)
            gs_s[rows, :] = moved[:, d:d + LANES] + moved[:, d + LANES:]
        y_s[...] = jnp.zeros(y_s.shape, F32)

    first_expert = e * EXPERTS_PER_STEP
    g = first_expert >> group_shift
    j0 = first_expert & (EXP_PER_GROUP - 1)
    bucket_start = meta_s[g]
    lane = lax.broadcasted_iota(jnp.int32, (GROUP_ALIGN, LANES), 1)

    def row_tile(i, _):
        rows = pl.ds(pl.multiple_of(bucket_start + i * GROUP_ALIGN, GROUP_ALIGN), GROUP_ALIGN)
        xs = hs_s[rows, :]
        gs = gs_s[rows, :]
        y = None
        for k in range(EXPERTS_PER_STEP):
            a = _dot(xs, wg_ref[k])
            bb = _dot(xs, wu_ref[k])
            g_e = jnp.sum(jnp.where(lane == j0 + k, gs, 0.0), axis=-1, keepdims=True)
            act = (a / (1.0 + jnp.exp(-a))) * bb * g_e
            contrib = _dot(act.astype(BF16), wd_ref[k])
            y = contrib if y is None else y + contrib
        y_s[rows, :] += y
        return 0

    lax.fori_loop(0, meta_s[N_GROUPS + g], row_tile, 0)

    @pl.when(e == pl.num_programs(2) - 1)
    def _():
        gt_m = mod_ref[:, 5:6, :]
        y = _tn_dot(perm_s[...], y_s[...].astype(BF16))
        out_ref[...] += gt_m * y.reshape(nb, ts, d)


def _post(x, o, mod, w, nb, ts):
    b, s, d = x.shape
    r = nb * ts
    n_rows = r + N_GROUPS * GROUP_ALIGN
    n_rows = -(-n_rows // 256) * 256
    ltri = (jnp.arange(r)[:, None] > jnp.arange(r)[None, :]).astype(BF16)
    tok = lambda width: pl.BlockSpec((nb, ts, width), lambda i, j, e: (i, j, 0))
    const = lambda a: pl.BlockSpec(a.shape, lambda i, j, e: (0,) * a.ndim, pipeline_mode=pl.Buffered(1))
    per_e = lambda a: pl.BlockSpec((EXPERTS_PER_STEP,) + a.shape[1:], lambda i, j, e: (e, 0, 0))
    return pl.pallas_call(
        _post_kernel,
        out_shape=jax.ShapeDtypeStruct((b, s, d), F32),
        grid=(b // nb, s // ts, N_EXPERTS // EXPERTS_PER_STEP),
        in_specs=[pl.BlockSpec((nb, ts, d), lambda i, j, e: (i, j, 0), pipeline_mode=pl.Buffered(1)),
                  pl.BlockSpec((nb, ts, o.shape[-1]), lambda i, j, e: (i, j, 0), pipeline_mode=pl.Buffered(1)),
                  pl.BlockSpec((nb, 6, d), lambda i, j, e: (i, 0, 0)),
                  const(w["w_out"]), const(w["g_ffn"]), const(w["w_r"]), const(w["b_r"]), const(ltri),
                  per_e(w["w_gate"]), per_e(w["w_up"]), per_e(w["w_down"])],
        out_specs=pl.BlockSpec((nb, ts, d), lambda i, j, e: (i, j, 0), pipeline_mode=pl.Buffered(1)),
        scratch_shapes=[pltpu.VMEM((n_rows, d), BF16), pltpu.VMEM((n_rows, LANES), F32),
                        pltpu.VMEM((n_rows, r), BF16), pltpu.VMEM((n_rows, d), F32),
                        pltpu.SMEM((2 * N_GROUPS,), jnp.int32)],
        name="post",
        compiler_params=pltpu.CompilerParams(dimension_semantics=("arbitrary", "arbitrary", "arbitrary"),
                                             vmem_limit_bytes=VMEM_LIMIT),
    )(x, o, mod, w["w_out"], w["g_ffn"], w["w_r"], w["b_r"], ltri, w["w_gate"], w["w_up"], w["w_down"])


def _block_mean(width, seg):
    idx = jnp.arange(width) // seg
    return jnp.where(idx[:, None] == idx[None, :], 1.0 / seg, 0.0).astype(BF16)


def _layer_weights(l, g_mix, w_in, g_sb_q, g_sb_k, g_qa, w_qb, g_kva, w_kvb, g_mq_nope, g_mk_nope, g_mq_pe,
                   g_mk_pe, w_out, g_ffn, w_rg, b_rg, w_re, b_re, w_gate, w_up, w_down):
    d = w_in.shape[1]
    row = lambda g, reps: jnp.tile(g[l], reps).reshape(1, -1)
    kpe_cols = jnp.tile(w_in[l][:, C_KPE:C_KPE + MLA_ROPE], (1, KPE_TILE))
    w_qb3 = w_qb[l].reshape(Q_RANK, MLA_HEADS, MLA_NOPE + MLA_ROPE)
    w_kvb3 = w_kvb[l].reshape(KV_RANK, MLA_HEADS, MLA_NOPE + MLA_V)
    pad = jnp.zeros((d, LANES), F32)
    w_r = jnp.concatenate([pad.at[:, :N_EXPERTS].set(w_re[l]), pad.at[:, :N_GROUPS].set(w_rg[l])], axis=1)
    zrow = jnp.zeros((LANES,), F32)
    b_r = jnp.concatenate([zrow.at[:N_EXPERTS].set(b_re[l]), zrow.at[:N_GROUPS].set(b_rg[l])]).reshape(1, -1)
    rep = (jnp.arange(MLA_ROPE)[:, None] == (jnp.arange(LANES)[None, :] % MLA_ROPE)).astype(BF16)
    return dict(
        g_mix=row(g_mix, 1),
        w_in=jnp.concatenate([w_in[l][:, :C_KPE], kpe_cols], axis=1).astype(BF16),
        g_sb_q=row(g_sb_q, SB_HEADS), g_sb_k=row(g_sb_k, SB_HEADS), g_qa=row(g_qa, 1),
        w_qn=w_qb3[:, :, :MLA_NOPE].reshape(Q_RANK, MLA_W).astype(BF16),
        w_qp=w_qb3[:, :, MLA_NOPE:].reshape(Q_RANK, PE_W).astype(BF16),
        g_kva=row(g_kva, 1),
        w_kk=w_kvb3[:, :, :MLA_NOPE].reshape(KV_RANK, MLA_W).astype(BF16),
        w_kv=w_kvb3[:, :, MLA_NOPE:].reshape(KV_RANK, MLA_W).astype(BF16),
        g_mq_nope=row(g_mq_nope, MLA_HEADS), g_mk_nope=row(g_mk_nope, MLA_HEADS),
        g_mq_pe=row(g_mq_pe, MLA_HEADS), g_mk_pe=row(g_mk_pe, KPE_TILE),
        m64=_block_mean(SB_W, SB_DIM), m32=_block_mean(PE_W, MLA_ROPE), rep=rep,
        w_out=w_out[l].astype(BF16), g_ffn=row(g_ffn, 1), w_r=w_r.astype(BF16), b_r=b_r,
        w_gate=w_gate[l].astype(BF16), w_up=w_up[l].astype(BF16), w_down=w_down[l].astype(BF16),
    )


def _rope_tables(pos):
    inv = ROPE_THETA ** (-jnp.arange(0, MLA_ROPE, 2, dtype=F32) / MLA_ROPE)
    ang = pos.astype(F32)[:, None] * inv[None, :]
    cos, sin = jnp.cos(ang), jnp.sin(ang)
    cos = jnp.tile(jnp.concatenate([cos, cos], axis=-1), (1, MLA_HEADS))
    sin = jnp.tile(jnp.concatenate([-sin, sin], axis=-1), (1, MLA_HEADS))
    return cos[None], sin[None]


def _largest_tile(n, cap):
    t = min(n, cap)
    while n % t:
        t -= 8
    return t


def kernel(x_prompt, x_sample, cache_sb_k, cache_sb_v, cache_mla_ckv, cache_mla_kpe, c_prompt, c_sample, w_ada, b_ada, g_mix, w_in, g_sb_q, g_sb_k, g_qa, w_qb, g_kva, w_kvb, g_mq_nope, g_mk_nope, g_mq_pe, g_mk_pe, w_out, g_ffn, w_rg, b_rg, w_re, b_re, w_gate, w_up, w_down):
    bp, sp, d = x_prompt.shape
    bs, ss, _ = x_sample.shape
    depth = w_ada.shape[0]
    past_len = cache_sb_k.shape[2]
    t_attn = 256
    t_cache = 512
    assert sp % t_attn == 0 and past_len % t_cache == 0 and ss % 8 == 0 and ss & (ss - 1) == 0

    mod = _ada(jnp.concatenate([c_prompt, c_sample], axis=0), w_ada, b_ada).reshape(depth, bp + bs, 6, d)
    cos_p, sin_p = _rope_tables(jnp.arange(sp, dtype=jnp.int32))
    cos_s, sin_s = _rope_tables(past_len + jnp.arange(ss, dtype=jnp.int32))
    tri = (jnp.arange(t_attn)[:, None] > jnp.arange(t_attn)[None, :]).astype(BF16)
    tri_u = (jnp.arange(t_cache)[:, None] < jnp.arange(t_cache)[None, :]).astype(BF16)
    sel = (jnp.arange(ss)[:, None] == (jnp.arange(SB_HEADS * ss)[None, :] % ss)).astype(BF16)
    cache_k = cache_sb_k.reshape(depth, bs, past_len, SB_W)
    cache_v = cache_sb_v.reshape(depth, bs, past_len, SB_W)

    ts_p = _largest_tile(sp, 512)
    ts_post = _largest_tile(sp, 1024)
    yp, ys = x_prompt, x_sample
    state_widths = (SB_W, SB_W, KV_RANK, MLA_ROPE)
    st_p = tuple(jnp.zeros((depth, bp, sp, wd), F32) for wd in state_widths)
    st_s = tuple(jnp.zeros((depth, bs, ss, wd), F32) for wd in state_widths)
    for l in range(depth):
        w = _layer_weights(l, g_mix, w_in, g_sb_q, g_sb_k, g_qa, w_qb, g_kva, w_kvb, g_mq_nope, g_mk_nope,
                           g_mq_pe, g_mk_pe, w_out, g_ffn, w_rg, b_rg, w_re, b_re, w_gate, w_up, w_down)
        mod_p, mod_s = mod[l, :bp], mod[l, bp:]

        st_p, (qsb, ksb, vsb, qn, qpe, kn, vm, kpe) = _pre(yp, mod_p, cos_p, sin_p, w, 1, ts_p, l, depth, st_p)
        o = _attn_prompt(qsb, qn, qpe, ksb, vsb, kn, kpe, vm, tri)
        yp = _post(yp, o, mod_p, w, 1, ts_post)

        st_s, (qsb, ksb, vsb, qn, qpe, kn, vm, kpe) = _pre(ys, mod_s, cos_s, sin_s, w, bs, ss, l, depth, st_s)
        o = _attn_sample(l, qsb, qn, qpe, ksb, vsb, kn, kpe, vm, cache_k, cache_v, cache_mla_ckv, cache_mla_kpe,
                         w, tri_u, sel)
        ys = _post(ys, o, mod_s, w, bs, ss)

    heads = lambda a: a.reshape(*a.shape[:3], SB_HEADS, SB_DIM)
    return (yp, ys, heads(st_p[0]), heads(st_p[1]), st_p[2], st_p[3],
            heads(st_s[0]), heads(st_s[1]), st_s[2], st_s[3])
```
